```python
import math
import jax, jax.numpy as jnp
from jax import lax
import numpy as np

D_MODEL = 1024
BATCH = 4
SEQ = 8192
DEPTH = 4
DEC_BATCH = 8
DEC_SEQ = 2048
PAST_LEN = 128

GRID_W = 64
CHUNK = 128
A_GROUPS = 4
A_WIDTH = 512
A_GC = A_WIDTH // A_GROUPS
POOL_WINDOWS = (2, 4, 8, 16)
B_WIDTH = 512
B_GC = B_WIDTH // len(POOL_WINDOWS)
N_HEADS = 8
N_KV_HEADS = 2
HEAD_DIM = 64
C_WIDTH = N_HEADS * HEAD_DIM
KV_WIDTH = N_KV_HEADS * HEAD_DIM
Q_BLOCK = 128
ROPE_THETA = 10000.0
ROPE_PAIRS_AXIS = HEAD_DIM // 4
D_WIDTH = 512
CONV_W = 3
N_BRANCHES = 4
A_COLS = 2 * A_WIDTH
B_COLS = B_WIDTH
C_COLS = C_WIDTH + 2 * KV_WIDTH
D_COLS = 3 * D_WIDTH
G_COLS = N_BRANCHES * D_MODEL
OFF_A = 0
OFF_B = OFF_A + A_COLS
OFF_C = OFF_B + B_COLS
OFF_D = OFF_C + C_COLS
OFF_G = OFF_D + D_COLS
N_IN = OFF_G + G_COLS
N_EXPERTS = 16
D_FF_EXPERT = 1024
EC_CAPACITY = 2
ALPHA = (2 * DEPTH) ** 0.25
BETA = (8 * DEPTH) ** -0.25
LN_EPS = 1e-5
RMS_EPS = 1e-6

kernel_name = 'hybrid_bidir_encoder_two_groups'


def _ln(x, g=None, b=None):
    x32 = x.astype(jnp.float32)
    xc = x32 - jnp.mean(x32, axis=-1, keepdims=True)
    y = xc * lax.rsqrt(jnp.mean(xc * xc, axis=-1, keepdims=True) + LN_EPS)
    if g is not None:
        y = y * g.astype(jnp.float32) + b.astype(jnp.float32)
    return y.astype(x.dtype)


def _rms(x, g):
    x32 = x.astype(jnp.float32)
    y = x32 * lax.rsqrt(jnp.mean(x32 * x32, axis=-1, keepdims=True) + RMS_EPS) * g.astype(jnp.float32)
    return y.astype(x.dtype)


def _axial_rope(T, dtype):
    n_rows = T // GRID_W
    row = jnp.repeat(jnp.arange(n_rows, dtype=jnp.float32), GRID_W)
    col = jnp.tile(jnp.arange(GRID_W, dtype=jnp.float32), n_rows)
    inv = ROPE_THETA ** (-jnp.arange(ROPE_PAIRS_AXIS, dtype=jnp.float32) / ROPE_PAIRS_AXIS)
    ang = jnp.concatenate([row[:, None] * inv, col[:, None] * inv], axis=-1)
    return jnp.cos(ang).astype(dtype), jnp.sin(ang).astype(dtype)


def _rope(x, cos, sin):
    xr = x.reshape(x.shape[:-1] + (HEAD_DIM // 2, 2))
    x0, x1 = xr[..., 0], xr[..., 1]
    c = cos[None, :, None, :]
    s = sin[None, :, None, :]
    return jnp.stack([x0 * c - x1 * s, x0 * s + x1 * c], axis=-1).reshape(x.shape)


def _spatial_gating(za, ln_g, ln_b, ws, bs):
    B, T, _ = za.shape
    za = jax.nn.gelu(za)
    u, v = za[..., :A_WIDTH], za[..., A_WIDTH:]
    v = _ln(v, ln_g, ln_b).reshape(B, T // CHUNK, CHUNK, A_GROUPS, A_GC)
    sv = jnp.einsum('gpq,bnqgc->bnpgc', ws, v) + bs.T[None, None, :, :, None]
    return u * sv.reshape(B, T, A_WIDTH)


def _multiscale_pool(zb, wg, scale):
    B, T, _ = zb.shape
    z32 = zb.astype(jnp.float32)
    cs = jnp.concatenate([jnp.zeros((B, 1, B_WIDTH), jnp.float32), jnp.cumsum(z32, axis=1)], axis=1)
    t = jnp.arange(T)
    outs = []
    for g, w in enumerate(POOL_WINDOWS):
        lo = jnp.clip(t - w // 2, 0, T)
        hi = jnp.clip(t + w // 2, 0, T)
        csg = cs[..., g * B_GC:(g + 1) * B_GC]
        s = jnp.take(csg, hi, axis=1) - jnp.take(csg, lo, axis=1)
        cnt = (hi - lo).astype(jnp.float32)[None, :, None]
        outs.append(s / cnt - z32[..., g * B_GC:(g + 1) * B_GC])
    pooled = jnp.stack(outs, axis=2).astype(zb.dtype)
    mixed = jnp.einsum('btgc,gcd->btgd', pooled, wg).reshape(B, T, B_WIDTH)
    return mixed * scale


def _attention(zc, q_g, k_g, cos, sin):
    B, T, _ = zc.shape
    q = zc[..., :C_WIDTH].reshape(B, T, N_HEADS, HEAD_DIM)
    k = zc[..., C_WIDTH:C_WIDTH + KV_WIDTH].reshape(B, T, N_KV_HEADS, HEAD_DIM)
    v = zc[..., C_WIDTH + KV_WIDTH:].reshape(B, T, N_KV_HEADS, HEAD_DIM)
    q = _rope(_rms(q, q_g), cos, sin)
    k = _rope(_rms(k, k_g), cos, sin)
    G = N_HEADS // N_KV_HEADS
    nb = T // Q_BLOCK
    qb = jnp.moveaxis(q.reshape(B, nb, Q_BLOCK, N_KV_HEADS, G, HEAD_DIM), 1, 0)
    scale = HEAD_DIM ** -0.5

    def block(qblk):
        s = jnp.einsum('bqkgd,bskd->bkgqs', qblk, k, preferred_element_type=jnp.float32) * scale
        p = jax.nn.softmax(s, axis=-1).astype(v.dtype)
        return jnp.einsum('bkgqs,bskd->bqkgd', p, v)

    o = lax.map(block, qb)
    return jnp.moveaxis(o, 0, 1).reshape(B, T, C_WIDTH)


def _short_conv(zd, conv_w):
    xd = zd[..., :D_WIDTH]
    bg = zd[..., D_WIDTH:2 * D_WIDTH]
    cg = zd[..., 2 * D_WIDTH:]
    z = cg * xd
    T = z.shape[1]
    pad = CONV_W // 2
    zp = jnp.pad(z, ((0, 0), (pad, pad), (0, 0)))
    y = zp[:, 0:T] * conv_w[0]
    for j in range(1, CONV_W):
        y = y + zp[:, j:j + T] * conv_w[j]
    return bg * y


def _expert_choice(h, r_w, r_b, w1, w3, w2):
    B, T, D = h.shape
    n_tok = B * T
    hf = h.reshape(n_tok, D)
    cap = EC_CAPACITY * n_tok // N_EXPERTS
    logits = jnp.dot(hf, r_w, preferred_element_type=jnp.float32) + r_b.astype(jnp.float32)
    aff = jax.nn.softmax(logits, axis=-1)
    gate, idx = lax.top_k(aff.T, cap)
    xg = hf[idx]
    a = jnp.einsum('ecd,edf->ecf', xg, w1)
    b = jnp.einsum('ecd,edf->ecf', xg, w3)
    out = jnp.einsum('ecf,efd->ecd', jax.nn.silu(a) * b, w2) * gate[..., None].astype(h.dtype)
    y = jnp.zeros_like(hf).at[idx.reshape(-1)].add(out.reshape(-1, D))
    return y.reshape(B, T, D)


def _trunk(x, c, w_ada, b_ada, w_in, a_ln_g, a_ln_b, a_ws, a_bs, b_wg, b_scale, c_q_g, c_k_g,
           d_conv, p_a, p_b, p_c, p_d, w_o, ln1_g, ln1_b, r_w, r_b, e_w1, e_w3, e_w2, ln2_g, ln2_b):
    B, T, _ = x.shape
    cos, sin = _axial_rope(T, x.dtype)
    for l in range(DEPTH):
        mod = (jnp.dot(jax.nn.silu(c), w_ada[l]) + b_ada[l])[:, None, :]
        sh1, sc1, g1, sh2, sc2, g2 = jnp.split(mod, 6, axis=-1)
        h = _ln(x) * (1 + sc1) + sh1
        z = jnp.dot(h, w_in[l])
        ya = _spatial_gating(z[..., OFF_A:OFF_B], a_ln_g[l], a_ln_b[l], a_ws[l], a_bs[l])
        yb = _multiscale_pool(z[..., OFF_B:OFF_C], b_wg[l], b_scale[l])
        yc = _attention(z[..., OFF_C:OFF_D], c_q_g[l], c_k_g[l], cos, sin)
        yd = _short_conv(z[..., OFF_D:OFF_G], d_conv[l])
        gates = jax.nn.sigmoid(z[..., OFF_G:].astype(jnp.float32)).astype(x.dtype)
        gates = gates.reshape(B, T, N_BRANCHES, D_MODEL)
        merged = (gates[:, :, 0] * jnp.dot(ya, p_a[l]) + gates[:, :, 1] * jnp.dot(yb, p_b[l])
                  + gates[:, :, 2] * jnp.dot(yc, p_c[l]) + gates[:, :, 3] * jnp.dot(yd, p_d[l]))
        y = jnp.dot(merged, w_o[l])
        x = _ln(ALPHA * x + g1 * y, ln1_g[l], ln1_b[l])
        h = _ln(x) * (1 + sc2) + sh2
        y = _expert_choice(h, r_w[l], r_b[l], e_w1[l], e_w3[l], e_w2[l])
        x = _ln(ALPHA * x + g2 * y, ln2_g[l], ln2_b[l])
    return x


def setup_inputs(seed: int = 0) -> dict:
    key = jax.random.key(seed)
    ks = iter(jax.random.split(key, 40))

    def nrm(shape, scale):
        return jax.random.normal(next(ks), shape, jnp.float32) * scale

    L, D, E, F = DEPTH, D_MODEL, N_EXPERTS, D_FF_EXPERT
    return {
        'x_prompt': nrm((BATCH, SEQ, D), 1.0),
        'x_sample': nrm((DEC_BATCH, DEC_SEQ, D), 1.0),
        'c_prompt': nrm((BATCH, D), 1.0),
        'c_sample': nrm((DEC_BATCH, D), 1.0),
        'w_ada': nrm((L, D, 6 * D), 0.5 * D ** -0.5),
        'b_ada': nrm((L, 6 * D), 0.02),
        'w_in': nrm((L, D, N_IN), D ** -0.5),
        'a_ln_g': 1.0 + nrm((L, A_WIDTH), 0.02),
        'a_ln_b': nrm((L, A_WIDTH), 0.02),
        'a_ws': nrm((L, A_GROUPS, CHUNK, CHUNK), CHUNK ** -0.5),
        'a_bs': 1.0 + nrm((L, A_GROUPS, CHUNK), 0.02),
        'b_wg': nrm((L, len(POOL_WINDOWS), B_GC, B_GC), B_GC ** -0.5),
        'b_scale': 1.0 + nrm((L, B_WIDTH), 0.02),
        'c_q_g': 1.0 + nrm((L, HEAD_DIM), 0.02),
        'c_k_g': 1.0 + nrm((L, HEAD_DIM), 0.02),
        'd_conv': nrm((L, CONV_W, D_WIDTH), CONV_W ** -0.5),
        'p_a': nrm((L, A_WIDTH, D), A_WIDTH ** -0.5),
        'p_b': nrm((L, B_WIDTH, D), B_WIDTH ** -0.5),
        'p_c': nrm((L, C_WIDTH, D), C_WIDTH ** -0.5),
        'p_d': nrm((L, D_WIDTH, D), D_WIDTH ** -0.5),
        'w_o': nrm((L, D, D), BETA * D ** -0.5),
        'ln1_g': 1.0 + nrm((L, D), 0.02),
        'ln1_b': nrm((L, D), 0.02),
        'r_w': nrm((L, D, E), D ** -0.5),
        'r_b': nrm((L, E), 0.01),
        'e_w1': nrm((L, E, D, F), D ** -0.5),
        'e_w3': nrm((L, E, D, F), D ** -0.5),
        'e_w2': nrm((L, E, F, D), BETA * F ** -0.5),
        'ln2_g': 1.0 + nrm((L, D), 0.02),
        'ln2_b': nrm((L, D), 0.02),
    }


def reference(x_prompt, x_sample, c_prompt, c_sample, w_ada, b_ada, w_in, a_ln_g, a_ln_b, a_ws, a_bs,
              b_wg, b_scale, c_q_g, c_k_g, d_conv, p_a, p_b, p_c, p_d, w_o, ln1_g, ln1_b, r_w, r_b,
              e_w1, e_w3, e_w2, ln2_g, ln2_b):
    y_prompt = _trunk(x_prompt, c_prompt, w_ada, b_ada, w_in, a_ln_g, a_ln_b, a_ws, a_bs, b_wg, b_scale,
                      c_q_g, c_k_g, d_conv, p_a, p_b, p_c, p_d, w_o, ln1_g, ln1_b, r_w, r_b,
                      e_w1, e_w3, e_w2, ln2_g, ln2_b)
    y_sample = _trunk(x_sample, c_sample, w_ada, b_ada, w_in, a_ln_g, a_ln_b, a_ws, a_bs, b_wg, b_scale,
                      c_q_g, c_k_g, d_conv, p_a, p_b, p_c, p_d, w_o, ln1_g, ln1_b, r_w, r_b,
                      e_w1, e_w3, e_w2, ln2_g, ln2_b)
    return (y_prompt, y_sample)
```

```python
import functools
from typing import NamedTuple

import numpy as np
import jax
import jax.numpy as jnp
from jax import lax
from jax.experimental import pallas as pl
from jax.experimental.pallas import tpu as pltpu

F32 = jnp.float32
BF16 = jnp.bfloat16
I32 = jnp.int32

D_MODEL = 1024
DEPTH = 4
GRID_W = 64
CHUNK = 128
A_GROUPS = 4
A_WIDTH = 512
POOL_WINDOWS = (2, 4, 8, 16)
B_WIDTH = 512
N_HEADS = 8
N_KV_HEADS = 2
HEAD_DIM = 64
C_WIDTH = N_HEADS * HEAD_DIM
KV_WIDTH = N_KV_HEADS * HEAD_DIM
ROPE_THETA = 10000.0
ROPE_PAIRS_AXIS = HEAD_DIM // 4
D_WIDTH = 512
CONV_W = 3
N_BRANCHES = 4
N_EXPERTS = 16
D_FF_EXPERT = 1024
EC_CAPACITY = 2
ALPHA = (2 * DEPTH) ** 0.25
LN_EPS = 1e-5
RMS_EPS = 1e-6

OFF_A = 0
OFF_B = OFF_A + 2 * A_WIDTH
OFF_C = OFF_B + B_WIDTH
OFF_D = OFF_C + C_WIDTH + 2 * KV_WIDTH
OFF_G = OFF_D + 3 * D_WIDTH
N_IN = OFF_G + N_BRANCHES * D_MODEL

LANES = 128
SUBLANES = 8
BF16_ROWS = 16
VMEM_LIMIT = 56 * 1024 * 1024

HP = LANES
HALO = SUBLANES


class Cfg(NamedTuple):
    B: int
    T: int
    TT: int
    TQ: int
    TKC: int
    W: int
    TM: int

    @property
    def n_tok(self): return self.B * self.T
    @property
    def nT(self): return self.T // self.TT
    @property
    def NT(self): return self.n_tok // self.TT
    @property
    def NTP(self): return -(-(self.NT + 1) // LANES) * LANES
    @property
    def cap(self): return EC_CAPACITY * self.n_tok // N_EXPERTS
    @property
    def capP(self):
        worst = self.cap + (BF16_ROWS - 1) * self.NT + self.W
        return -(-worst // self.TM) * self.TM


def make_cfg(B, T):
    return Cfg(B=B, T=T, TT=256, TQ=256, TKC=512, W=64, TM=256)


def _cparams(sem):
    return pltpu.CompilerParams(dimension_semantics=sem, vmem_limit_bytes=VMEM_LIMIT)


def _ln(x):
    mu = jnp.mean(x, axis=-1, keepdims=True)
    xc = x - mu
    var = jnp.mean(xc * xc, axis=-1, keepdims=True)
    return xc * lax.rsqrt(var + LN_EPS)


def _gelu_tanh(x):
    return 0.5 * x * (1.0 + jnp.tanh(np.sqrt(2.0 / np.pi).astype(np.float32) * (x + 0.044715 * (x * x * x))))


def _sigmoid(x):
    return 1.0 / (1.0 + jnp.exp(-x))


def _dot(a, b):
    return jnp.dot(a, b, preferred_element_type=F32)


def _once(block, index_map):
    return pl.BlockSpec(block, index_map, pipeline_mode=pl.Buffered(1))


def _mod_body(c_ref, w_ref, b_ref, o_ref):
    c = c_ref[...]
    s = (c * _sigmoid(c)).astype(BF16)
    o_ref[0] = _dot(s, w_ref[0].astype(BF16)) + b_ref[0]


def _modulation(c_all, w_ada, b_ada):
    R = c_all.shape[0]
    L = w_ada.shape[0]
    nb = 1536
    return pl.pallas_call(
        _mod_body,
        grid=(L, 6 * D_MODEL // nb),
        in_specs=[pl.BlockSpec((R, D_MODEL), lambda l, j: (0, 0)),
                  pl.BlockSpec((1, D_MODEL, nb), lambda l, j: (l, 0, j)),
                  pl.BlockSpec((1, 1, nb), lambda l, j: (l, 0, j))],
        out_specs=pl.BlockSpec((1, R, nb), lambda l, j: (l, 0, j)),
        out_shape=jax.ShapeDtypeStruct((L, R, 6 * D_MODEL), F32),
        compiler_params=_cparams(("arbitrary", "arbitrary")),
        name="modulation",
    )(c_all, w_ada, b_ada.reshape(L, 1, 6 * D_MODEL))


def _rms_rope(z, g, cosb, sinb, even):
    ms = jnp.sum(z * z, axis=-1, keepdims=True) * (1.0 / HEAD_DIM)
    zn = z * lax.rsqrt(ms + RMS_EPS) * g
    partner = jnp.where(even, pltpu.roll(zn, HP - 1, 1), pltpu.roll(zn, 1, 1))
    return zn * cosb + partner * sinb


def _in_proj_body(x_ref, mod_ref, wa_ref, wq_ref, wg_ref, wd_ref, wb_ref, wk_ref, wv_ref,
                  alng_ref, alnb_ref, aws_ref, absT_ref, qg_ref, kg_ref, cos_ref, sin_ref,
                  ya_ref, zb_ref, zp_ref, bg_ref, qp_ref, kT_ref, vp_ref, gates_ref, *, TT):
    x = x_ref[0]
    mod = mod_ref[0]
    sh1 = mod[:, 0:D_MODEL]
    sc1 = mod[:, D_MODEL:2 * D_MODEL]
    h = (_ln(x) * (1.0 + sc1) + sh1).astype(BF16)

    ga = _gelu_tanh(_dot(h, wa_ref[0]))
    u = ga[:, :A_WIDTH]
    vn = (_ln(ga[:, A_WIDTH:]) * alng_ref[0] + alnb_ref[0]).astype(BF16)
    bsT = absT_ref[0]
    gc = A_WIDTH // A_GROUPS
    for n in range(TT // CHUNK):
        rows = slice(n * CHUNK, (n + 1) * CHUNK)
        for g in range(A_GROUPS):
            cols = slice(g * gc, (g + 1) * gc)
            sv = _dot(aws_ref[0, g], vn[rows, cols]) + bsT[:, g:g + 1]
            ya_ref[0, rows, cols] = (u[rows, cols] * sv).astype(BF16)

    cosb = cos_ref[...]
    sinb = sin_ref[...]
    lane = lax.broadcasted_iota(I32, (TT, HP), 1)
    even = (lane & 1) == 0
    zq = _dot(h, wq_ref[0])
    for hh in range(N_HEADS):
        q = _rms_rope(zq[:, hh * HP:(hh + 1) * HP], qg_ref[0], cosb, sinb, even)
        qp_ref[0, :, hh * HP:(hh + 1) * HP] = (q * (HEAD_DIM ** -0.5)).astype(BF16)
    zk = _dot(h, wk_ref[0])
    zv = _dot(h, wv_ref[0])
    for kv in range(N_KV_HEADS):
        k = _rms_rope(zk[:, kv * HP:(kv + 1) * HP], kg_ref[0], cosb, sinb, even)
        kT_ref[0, kv] = k.T.astype(BF16)
        v = jnp.where(lane == HEAD_DIM, 1.0, zv[:, kv * HP:(kv + 1) * HP])
        vp_ref[0, kv] = v.astype(BF16)

    zd = _dot(h, wd_ref[0])
    zp_ref[0] = zd[:, 2 * D_WIDTH:] * zd[:, :D_WIDTH]
    bg_ref[0] = zd[:, D_WIDTH:2 * D_WIDTH]
    zb_ref[0] = _dot(h, wb_ref[0])
    for c in range(N_BRANCHES):
        cols = slice(c * D_MODEL, (c + 1) * D_MODEL)
        gates_ref[0, :, cols] = _sigmoid(_dot(h, wg_ref[0, :, cols]))


def _in_proj(cfg, l, x, mod, w_in_p, a_ln_g, a_ln_b, a_ws, a_bsT, q_g, k_g, cos_t, sin_t):
    B, T, TT, nT = cfg.B, cfg.T, cfg.TT, cfg.nT
    tok = lambda b, i: (b, i, 0)
    lay3 = lambda b, i: (l, 0, 0)
    wspec = lambda width, start: _once((1, D_MODEL, width), lambda b, i: (l, 0, start // width))
    in_specs = [
        pl.BlockSpec((1, TT, D_MODEL), tok),
        pl.BlockSpec((1, 1, 6 * D_MODEL), lambda b, i: (b, 0, 0)),
        wspec(1024, 4096), wspec(1024, 5120), wspec(4096, 0), wspec(1536, 6144),
        wspec(512, 7680), wspec(256, 8192), wspec(256, 8448),
        _once((1, 1, A_WIDTH), lay3), _once((1, 1, A_WIDTH), lay3),
        _once((1, A_GROUPS, CHUNK, CHUNK), lambda b, i: (l, 0, 0, 0)),
        _once((1, CHUNK, A_GROUPS), lay3),
        _once((1, 1, HP), lay3), _once((1, 1, HP), lay3),
        pl.BlockSpec((TT, HP), lambda b, i: (i, 0)), pl.BlockSpec((TT, HP), lambda b, i: (i, 0)),
    ]
    out_shape = [
        jax.ShapeDtypeStruct((B, T, A_WIDTH), BF16),
        jax.ShapeDtypeStruct((B, T, B_WIDTH), F32),
        jax.ShapeDtypeStruct((B, T, D_WIDTH), F32),
        jax.ShapeDtypeStruct((B, T, D_WIDTH), F32),
        jax.ShapeDtypeStruct((B, T, N_HEADS * HP), BF16),
        jax.ShapeDtypeStruct((B, N_KV_HEADS, HP, T), BF16),
        jax.ShapeDtypeStruct((B, N_KV_HEADS, T, HP), BF16),
        jax.ShapeDtypeStruct((B, T, N_BRANCHES * D_MODEL), F32),
    ]
    out_specs = [
        pl.BlockSpec((1, TT, A_WIDTH), tok), pl.BlockSpec((1, TT, B_WIDTH), tok),
        pl.BlockSpec((1, TT, D_WIDTH), tok), pl.BlockSpec((1, TT, D_WIDTH), tok),
        pl.BlockSpec((1, TT, N_HEADS * HP), tok),
        pl.BlockSpec((1, N_KV_HEADS, HP, TT), lambda b, i: (b, 0, 0, i)),
        pl.BlockSpec((1, N_KV_HEADS, TT, HP), lambda b, i: (b, 0, i, 0)),
        pl.BlockSpec((1, TT, N_BRANCHES * D_MODEL), tok),
    ]
    return pl.pallas_call(
        functools.partial(_in_proj_body, TT=TT),
        grid=(B, nT), in_specs=in_specs, out_specs=out_specs, out_shape=out_shape,
        compiler_params=_cparams(("parallel", "parallel")),
        name="in_proj",
    )(x, mod, w_in_p, w_in_p, w_in_p, w_in_p, w_in_p, w_in_p, w_in_p,
      a_ln_g, a_ln_b, a_ws, a_bsT, q_g, k_g, cos_t, sin_t)


def _attn_body(q_ref, kT_ref, v_ref, o_ref, m_sc, acc_sc, *, TQ, TKC, T):
    G = N_HEADS // N_KV_HEADS
    lane = lax.broadcasted_iota(I32, (TQ, HP), 1)
    for kv in range(N_KV_HEADS):
        qg = jnp.concatenate([q_ref[0, :, (kv * G + g) * HP:(kv * G + g + 1) * HP] for g in range(G)], axis=0)
        m_sc[...] = jnp.full(m_sc.shape, -jnp.inf, F32)
        acc_sc[...] = jnp.zeros(acc_sc.shape, F32)

        def step(j, carry):
            off = pl.multiple_of(j * TKC, TKC)
            s = _dot(qg, kT_ref[0, kv, :, pl.ds(off, TKC)])
            m_prev = m_sc[...]
            m_new = jnp.maximum(m_prev, jnp.max(s, axis=1, keepdims=True))
            p = jnp.exp(s - m_new).astype(BF16)
            acc_sc[...] = jnp.exp(m_prev - m_new) * acc_sc[...] + _dot(p, v_ref[0, kv, pl.ds(off, TKC), :])
            m_sc[...] = m_new
            return carry

        lax.fori_loop(0, T // TKC, step, 0)
        acc = acc_sc[...]
        o = acc / acc[:, HEAD_DIM:HEAD_DIM + 1]
        for pp in range(G // 2):
            o_even = o[(2 * pp) * TQ:(2 * pp + 1) * TQ]
            o_odd = o[(2 * pp + 1) * TQ:(2 * pp + 2) * TQ]
            pair = jnp.where(lane < HEAD_DIM, o_even, pltpu.roll(o_odd, HEAD_DIM, 1))
            blk = kv * (G // 2) + pp
            o_ref[0, :, blk * HP:(blk + 1) * HP] = pair.astype(BF16)


def _attention(cfg, qp, kT, vp):
    B, T, TQ, TKC = cfg.B, cfg.T, cfg.TQ, cfg.TKC
    G = N_HEADS // N_KV_HEADS
    return pl.pallas_call(
        functools.partial(_attn_body, TQ=TQ, TKC=TKC, T=T),
        grid=(B, T // TQ),
        in_specs=[pl.BlockSpec((1, TQ, N_HEADS * HP), lambda b, i: (b, i, 0)),
                  pl.BlockSpec((1, N_KV_HEADS, HP, T), lambda b, i: (b, 0, 0, 0)),
                  pl.BlockSpec((1, N_KV_HEADS, T, HP), lambda b, i: (b, 0, 0, 0))],
        out_specs=pl.BlockSpec((1, TQ, C_WIDTH), lambda b, i: (b, i, 0)),
        out_shape=jax.ShapeDtypeStruct((B, T, C_WIDTH), BF16),
        scratch_shapes=[pltpu.VMEM((G * TQ, 1), F32), pltpu.VMEM((G * TQ, HP), F32)],
        compiler_params=_cparams(("parallel", "arbitrary")),
        name="attention",
    )(qp, kT, vp)


def _merge_body(ya_ref, yc_ref, zb_ref, zb_prev_ref, zb_next_ref, zp_ref, zp_prev_ref, zp_next_ref,
                bg_ref, gates_ref, x_ref, mod_ref, bwg_ref, bscale_ref, dconv_ref,
                pa_ref, pb_ref, pc_ref, pd_ref, wo_ref, ln1g_ref, ln1b_ref, rwT_ref, rb_ref,
                x1_ref, h2_ref, aff_ref, *, TT, T):
    i = pl.program_id(1)
    first = i == 0
    last = i == pl.num_programs(1) - 1
    R = TT + 2 * HALO

    def with_halo(cur_ref, prev_ref, next_ref):
        prev = jnp.where(first, 0.0, prev_ref[0])
        nxt = jnp.where(last, 0.0, next_ref[0])
        return jnp.concatenate([prev, cur_ref[0], nxt], axis=0)

    up = lambda a, k: pltpu.roll(a, k, 0)
    down = lambda a, k: pltpu.roll(a, R - k, 0)

    zbe = with_halo(zb_ref, zb_prev_ref, zb_next_ref)
    t = i * TT + lax.broadcasted_iota(I32, (TT, 1), 0)
    gc = B_WIDTH // len(POOL_WINDOWS)
    mixed = []
    for g, w in enumerate(POOL_WINDOWS):
        e = zbe[:, g * gc:(g + 1) * gc]
        s = up(e, 1) + e
        half = 1
        while 2 * half < w:
            s = up(s, half) + down(s, half)
            half *= 2
        cnt = (jnp.minimum(t + w // 2, T) - jnp.maximum(t - w // 2, 0)).astype(F32)
        pooled = s[HALO:HALO + TT] / cnt - e[HALO:HALO + TT]
        mixed.append(_dot(pooled.astype(BF16), bwg_ref[0, g]))
    yb = jnp.concatenate(mixed, axis=1) * bscale_ref[0]

    zpe = with_halo(zp_ref, zp_prev_ref, zp_next_ref)
    cw = dconv_ref[0]
    conv = up(zpe, 1) * cw[0:1] + zpe * cw[1:2] + down(zpe, 1) * cw[2:3]
    yd = bg_ref[0] * conv[HALO:HALO + TT]

    merged = gates_ref[0, :, 0:D_MODEL] * _dot(ya_ref[0], pa_ref[0])
    merged += gates_ref[0, :, D_MODEL:2 * D_MODEL] * _dot(yb.astype(BF16), pb_ref[0])
    merged += gates_ref[0, :, 2 * D_MODEL:3 * D_MODEL] * _dot(yc_ref[0], pc_ref[0])
    merged += gates_ref[0, :, 3 * D_MODEL:4 * D_MODEL] * _dot(yd.astype(BF16), pd_ref[0])
    y = _dot(merged.astype(BF16), wo_ref[0])

    mod = mod_ref[0]
    g1 = mod[:, 2 * D_MODEL:3 * D_MODEL]
    sh2 = mod[:, 3 * D_MODEL:4 * D_MODEL]
    sc2 = mod[:, 4 * D_MODEL:5 * D_MODEL]
    x1 = _ln(ALPHA * x_ref[0] + g1 * y) * ln1g_ref[0] + ln1b_ref[0]
    x1_ref[0] = x1

    h2 = (_ln(x1) * (1.0 + sc2) + sh2).astype(BF16)
    h2_ref[0] = h2
    logits = lax.dot_general(rwT_ref[0], h2, (((1,), (1,)), ((), ())), preferred_element_type=F32) + rb_ref[0]
    ex = jnp.exp(logits - jnp.max(logits, axis=0, keepdims=True))
    aff_ref[0] = ex / jnp.sum(ex, axis=0, keepdims=True)


def _merge(cfg, l, ya, yc, zb, zp, bg, gates, x, mod, b_wg, b_scale, d_conv, p_a, p_b, p_c, p_d, w_o,
           ln1_g, ln1_b, r_wT, r_b):
    B, T, TT, nT = cfg.B, cfg.T, cfg.TT, cfg.nT
    hb = TT // HALO
    tok = lambda b, i: (b, i, 0)
    prev = lambda b, i: (b, jnp.maximum(i * hb - 1, 0), 0)
    nxt = lambda b, i: (b, jnp.minimum((i + 1) * hb, T // HALO - 1), 0)
    lay3 = lambda b, i: (l, 0, 0)
    in_specs = [
        pl.BlockSpec((1, TT, A_WIDTH), tok), pl.BlockSpec((1, TT, C_WIDTH), tok),
        pl.BlockSpec((1, TT, B_WIDTH), tok), pl.BlockSpec((1, HALO, B_WIDTH), prev), pl.BlockSpec((1, HALO, B_WIDTH), nxt),
        pl.BlockSpec((1, TT, D_WIDTH), tok), pl.BlockSpec((1, HALO, D_WIDTH), prev), pl.BlockSpec((1, HALO, D_WIDTH), nxt),
        pl.BlockSpec((1, TT, D_WIDTH), tok),
        pl.BlockSpec((1, TT, N_BRANCHES * D_MODEL), tok),
        pl.BlockSpec((1, TT, D_MODEL), tok),
        pl.BlockSpec((1, 1, 6 * D_MODEL), lambda b, i: (b, 0, 0)),
        _once((1, len(POOL_WINDOWS), LANES, LANES), lambda b, i: (l, 0, 0, 0)),
        _once((1, 1, B_WIDTH), lay3), _once((1, CONV_W, D_WIDTH), lay3),
        _once((1, A_WIDTH, D_MODEL), lay3), _once((1, B_WIDTH, D_MODEL), lay3),
        _once((1, C_WIDTH, D_MODEL), lay3), _once((1, D_WIDTH, D_MODEL), lay3),
        _once((1, D_MODEL, D_MODEL), lay3),
        _once((1, 1, D_MODEL), lay3), _once((1, 1, D_MODEL), lay3),
        _once((1, N_EXPERTS, D_MODEL), lay3), _once((1, N_EXPERTS, 1), lay3),
    ]
    out_shape = [jax.ShapeDtypeStruct((B, T, D_MODEL), F32),
                 jax.ShapeDtypeStruct((B, T, D_MODEL), BF16),
                 jax.ShapeDtypeStruct((cfg.NT, N_EXPERTS, TT), F32)]
    out_specs = [pl.BlockSpec((1, TT, D_MODEL), tok), pl.BlockSpec((1, TT, D_MODEL), tok),
                 pl.BlockSpec((1, N_EXPERTS, TT), lambda b, i: (b * nT + i, 0, 0))]
    return pl.pallas_call(
        functools.partial(_merge_body, TT=TT, T=T),
        grid=(B, nT), in_specs=in_specs, out_specs=out_specs, out_shape=out_shape,
        compiler_params=_cparams(("parallel", "arbitrary")),
        name="merge",
    )(ya, yc, zb, zb, zb, zp, zp, zp, bg, gates, x, mod, b_wg, b_scale, d_conv,
      p_a, p_b, p_c, p_d, w_o, ln1_g, ln1_b, r_wT, r_b)


def _selected(aff_tile, tau, tie_end, tile, TT):
    bits = pltpu.bitcast(aff_tile, I32)
    tok = tile * TT + lax.broadcasted_iota(I32, bits.shape, 1)
    return (bits > tau) | ((bits == tau) & (tok < tie_end))


def _count(mask):
    return jnp.sum(jnp.sum(jnp.where(mask, 1.0, 0.0), axis=0), axis=1, keepdims=True)


def _select_body(aff_ref, tauP_ref, meta_ref, *, NT, TT, NTP, cap):
    bits = pltpu.bitcast(aff_ref[...], I32)
    capf = float(cap)

    def tau_step(k, prefix):
        cand = prefix | jnp.left_shift(jnp.int32(1), 30 - k)
        return jnp.where(_count(bits >= cand[None]) >= capf, cand, prefix)
    tau = lax.fori_loop(0, 31, tau_step, jnp.zeros((N_EXPERTS, 1), I32))

    need = capf - _count(bits > tau[None])
    tie = bits == tau[None]
    tok = (lax.broadcasted_iota(I32, bits.shape, 0) * TT + lax.broadcasted_iota(I32, bits.shape, 2))
    n_bits = int(NT * TT).bit_length()

    def tie_step(k, end):
        cand = end | jnp.left_shift(jnp.int32(1), n_bits - 1 - k)
        return jnp.where(_count(tie & (tok < cand[None])) <= need, cand, end)
    tie_end = lax.fori_loop(0, n_bits, tie_step, jnp.zeros((N_EXPERTS, 1), I32))

    lane = lax.broadcasted_iota(I32, (N_EXPERTS, LANES), 1)
    tauP_ref[...] = jnp.where(lane == 0, tau, jnp.where(lane == 1, tie_end, 0))

    col = lax.broadcasted_iota(I32, (N_EXPERTS, NTP), 1)

    def cnt_step(i, acc):
        sel = _selected(aff_ref[i], tau, tie_end, i, TT)
        c = jnp.sum(jnp.where(sel, 1.0, 0.0), axis=1, keepdims=True)
        return jnp.where(col == i, c, acc)
    counts = lax.fori_loop(0, NT, cnt_step, jnp.zeros((N_EXPERTS, NTP), F32)).astype(I32)
    padded = ((counts + (BF16_ROWS - 1)) // BF16_ROWS) * BF16_ROWS
    before = (lax.broadcasted_iota(I32, (NTP, NTP), 0) < lax.broadcasted_iota(I32, (NTP, NTP), 1))
    offs = _dot(padded.astype(BF16), jnp.where(before, 1.0, 0.0).astype(BF16))
    meta_ref[0:N_EXPERTS, :] = offs.astype(I32)
    meta_ref[N_EXPERTS:2 * N_EXPERTS, :] = counts


def _select(cfg, aff3):
    NT, TT, NTP = cfg.NT, cfg.TT, cfg.NTP
    return pl.pallas_call(
        functools.partial(_select_body, NT=NT, TT=TT, NTP=NTP, cap=cfg.cap),
        grid=(1,),
        in_specs=[pl.BlockSpec((NT, N_EXPERTS, TT), lambda i: (0, 0, 0))],
        out_specs=[pl.BlockSpec((N_EXPERTS, LANES), lambda i: (0, 0)),
                   pl.BlockSpec((2 * N_EXPERTS, NTP), lambda i: (0, 0))],
        out_shape=[jax.ShapeDtypeStruct((N_EXPERTS, LANES), I32),
                   jax.ShapeDtypeStruct((2 * N_EXPERTS, NTP), I32)],
        compiler_params=_cparams(("arbitrary",)),
        name="select",
    )(aff3)


def _tile_onehots(aff_ref, tauP_ref, tile, r, TT, W):
    tau = tauP_ref[:, 0:1]
    tie_end = tauP_ref[:, 1:2]
    sel = _selected(aff_ref[0], tau, tie_end, tile, TT)
    before = (lax.broadcasted_iota(I32, (TT, TT), 0) < lax.broadcasted_iota(I32, (TT, TT), 1))
    rank = _dot(jnp.where(sel, 1.0, 0.0).astype(BF16), jnp.where(before, 1.0, 0.0).astype(BF16))
    slot = (r * W + lax.broadcasted_iota(I32, (W, TT), 0)).astype(F32)
    hots = []
    for e in range(N_EXPERTS):
        hit = sel[e:e + 1, :] & (rank[e:e + 1, :] == slot)
        hots.append(jnp.where(hit, 1.0, 0.0))
    return jnp.concatenate(hots, axis=0)


def _rounds(meta_ref, tile, W):
    n_max = meta_ref[N_EXPERTS, tile]
    for e in range(1, N_EXPERTS):
        n_max = jnp.maximum(n_max, meta_ref[N_EXPERTS + e, tile])
    return (n_max + (W - 1)) // W


def _window_live(meta_ref, e, tile, r, W):
    n = meta_ref[N_EXPERTS + e, tile]
    padded = ((n + (BF16_ROWS - 1)) // BF16_ROWS) * BF16_ROWS
    return r * W < padded


def _window_start(meta_ref, e, tile, r, W, capP):
    return pl.multiple_of(e * capP + meta_ref[e, tile] + r * W, BF16_ROWS)


def _dispatch_body(meta_ref, aff_ref, tauP_ref, h_ref, xg_in_ref, xg_ref, stage, sem, *, TT, W, capP):
    del xg_in_ref
    i = pl.program_id(0)
    slot = i % 2

    def copies(tile, r, buf):
        out = []
        for e in range(N_EXPERTS):
            cp = pltpu.make_async_copy(stage.at[buf, pl.ds(e * W, W)],
                                       xg_ref.at[pl.ds(_window_start(meta_ref, e, tile, r, W, capP), W)],
                                       sem.at[buf])
            out.append((_window_live(meta_ref, e, tile, r, W), cp))
        return out

    def start(tile, r, buf):
        for live, cp in copies(tile, r, buf):
            @pl.when(live)
            def _():
                cp.start()

    def wait(tile, r, buf):
        for live, cp in copies(tile, r, buf):
            @pl.when(live)
            def _():
                cp.wait()

    nr = _rounds(meta_ref, i, W)
    nr_prev = _rounds(meta_ref, jnp.maximum(i - 1, 0), W)
    prev_pending = (i > 0) & (nr_prev > 0)

    def one_round(r, carry):
        @pl.when(r > 0)
        def _():
            wait(i, r - 1, slot)
        hot = _tile_onehots(aff_ref, tauP_ref, i, r, TT, W).astype(BF16)
        stage[slot] = _dot(hot, h_ref[...]).astype(BF16)

        @pl.when((r == 0) & prev_pending)
        def _():
            wait(i - 1, nr_prev - 1, 1 - slot)
        start(i, r, slot)
        return carry

    lax.fori_loop(0, nr, one_round, 0)

    @pl.when((nr == 0) & prev_pending)
    def _():
        wait(i - 1, nr_prev - 1, 1 - slot)

    @pl.when((i == pl.num_programs(0) - 1) & (nr > 0))
    def _():
        wait(i, nr - 1, slot)


def _dispatch(cfg, meta, aff3, tauP, h2):
    NT, TT, W, capP = cfg.NT, cfg.TT, cfg.W, cfg.capP
    xg0 = jnp.zeros((N_EXPERTS * capP, D_MODEL), BF16)
    grid_spec = pltpu.PrefetchScalarGridSpec(
        num_scalar_prefetch=1, grid=(NT,),
        in_specs=[pl.BlockSpec((1, N_EXPERTS, TT), lambda i, m: (i, 0, 0)),
                  pl.BlockSpec((N_EXPERTS, LANES), lambda i, m: (0, 0)),
                  pl.BlockSpec((TT, D_MODEL), lambda i, m: (i, 0)),
                  pl.BlockSpec(memory_space=pl.ANY)],
        out_specs=pl.BlockSpec(memory_space=pl.ANY),
        scratch_shapes=[pltpu.VMEM((2, N_EXPERTS * W, D_MODEL), BF16), pltpu.SemaphoreType.DMA((2,))])
    return pl.pallas_call(
        functools.partial(_dispatch_body, TT=TT, W=W, capP=capP),
        grid_spec=grid_spec,
        out_shape=jax.ShapeDtypeStruct((N_EXPERTS * capP, D_MODEL), BF16),
        input_output_aliases={4: 0},
        compiler_params=_cparams(("arbitrary",)),
        name="dispatch",
    )(meta, aff3, tauP, h2, xg0)


def _experts_body(meta_ref, xg_ref, w1_ref, w3_ref, w2_ref, o_ref, *, NT, TM, W):
    e = pl.program_id(0)
    k = pl.program_id(1)
    used = meta_ref[e, NT] + W

    @pl.when(k * TM < used)
    def _():
        x = xg_ref[...]
        a = _dot(x, w1_ref[0, 0])
        b = _dot(x, w3_ref[0, 0])
        o_ref[...] = _dot((a * _sigmoid(a) * b).astype(BF16), w2_ref[0, 0])

    @pl.when(k * TM >= used)
    def _():
        o_ref[...] = jnp.zeros(o_ref.shape, F32)


def _experts(cfg, l, meta, xg, w1, w3, w2):
    TM, capP = cfg.TM, cfg.capP
    nk = capP // TM
    wspec = pl.BlockSpec((1, 1, D_MODEL, D_FF_EXPERT), lambda e, k, m: (l, e, 0, 0))
    grid_spec = pltpu.PrefetchScalarGridSpec(
        num_scalar_prefetch=1, grid=(N_EXPERTS, nk),
        in_specs=[pl.BlockSpec((TM, D_MODEL), lambda e, k, m: (e * nk + k, 0)), wspec, wspec,
                  pl.BlockSpec((1, 1, D_FF_EXPERT, D_MODEL), lambda e, k, m: (l, e, 0, 0))],
        out_specs=pl.BlockSpec((TM, D_MODEL), lambda e, k, m: (e * nk + k, 0)))
    return pl.pallas_call(
        functools.partial(_experts_body, NT=cfg.NT, TM=TM, W=cfg.W),
        grid_spec=grid_spec,
        out_shape=jax.ShapeDtypeStruct((N_EXPERTS * capP, D_MODEL), F32),
        compiler_params=_cparams(("arbitrary", "arbitrary")),
        name="experts",
    )(meta, xg, w1, w3, w2)


def _combine_body(meta_ref, aff_ref, tauP_ref, x1_ref, mod_ref, ln2g_ref, ln2b_ref, og_ref,
                  x2_ref, rows, acc, sem, *, TT, W, capP):
    i = pl.program_id(0)

    def copies(r):
        out = []
        for e in range(N_EXPERTS):
            cp = pltpu.make_async_copy(og_ref.at[pl.ds(_window_start(meta_ref, e, i, r, W, capP), W)],
                                       rows.at[pl.ds(e * W, W)], sem.at[0])
            out.append((_window_live(meta_ref, e, i, r, W), cp))
        return out

    acc[...] = jnp.zeros(acc.shape, F32)
    aff = aff_ref[0]

    def one_round(r, carry):
        for e, (live, cp) in enumerate(copies(r)):
            @pl.when(live)
            def _():
                cp.start()

            @pl.when(jnp.logical_not(live))
            def _():
                rows[pl.ds(e * W, W), :] = jnp.zeros((W, D_MODEL), F32)
        hot = _tile_onehots(aff_ref, tauP_ref, i, r, TT, W)
        gate = jnp.concatenate(
            [jnp.sum(hot[e * W:(e + 1) * W] * aff[e:e + 1, :], axis=1, keepdims=True) for e in range(N_EXPERTS)], axis=0)
        for live, cp in copies(r):
            @pl.when(live)
            def _():
                cp.wait()
        scaled = rows[...] * gate
        hi = scaled.astype(BF16)
        lo = (scaled - hi.astype(F32)).astype(BF16)
        hot16 = hot.astype(BF16)
        tn = (((0,), (0,)), ((), ()))
        acc[...] += (lax.dot_general(hot16, hi, tn, preferred_element_type=F32)
                     + lax.dot_general(hot16, lo, tn, preferred_element_type=F32))
        return carry

    lax.fori_loop(0, _rounds(meta_ref, i, W), one_round, 0)
    g2 = mod_ref[0][:, 5 * D_MODEL:6 * D_MODEL]
    x2_ref[0] = _ln(ALPHA * x1_ref[0] + g2 * acc[...]) * ln2g_ref[0] + ln2b_ref[0]


def _combine(cfg, l, meta, aff3, tauP, x1, mod, ln2_g, ln2_b, og):
    B, T, TT, nT, NT, W, capP = cfg.B, cfg.T, cfg.TT, cfg.nT, cfg.NT, cfg.W, cfg.capP
    tok = lambda i, m: (i // nT, i % nT, 0)
    lay3 = lambda i, m: (l, 0, 0)
    grid_spec = pltpu.PrefetchScalarGridSpec(
        num_scalar_prefetch=1, grid=(NT,),
        in_specs=[pl.BlockSpec((1, N_EXPERTS, TT), lambda i, m: (i, 0, 0)),
                  pl.BlockSpec((N_EXPERTS, LANES), lambda i, m: (0, 0)),
                  pl.BlockSpec((1, TT, D_MODEL), tok),
                  pl.BlockSpec((1, 1, 6 * D_MODEL), lambda i, m: (i // nT, 0, 0)),
                  pl.BlockSpec((1, 1, D_MODEL), lay3), pl.BlockSpec((1, 1, D_MODEL), lay3),
                  pl.BlockSpec(memory_space=pl.ANY)],
        out_specs=pl.BlockSpec((1, TT, D_MODEL), tok),
        scratch_shapes=[pltpu.VMEM((N_EXPERTS * W, D_MODEL), F32), pltpu.VMEM((TT, D_MODEL), F32),
                        pltpu.SemaphoreType.DMA((1,))])
    return pl.pallas_call(
        functools.partial(_combine_body, TT=TT, W=W, capP=capP),
        grid_spec=grid_spec,
        out_shape=jax.ShapeDtypeStruct((B, T, D_MODEL), F32),
        compiler_params=_cparams(("arbitrary",)),
        name="combine",
    )(meta, aff3, tauP, x1, mod, ln2_g, ln2_b, og)


def _rope_tables(T):
    n_rows = T // GRID_W
    row = jnp.repeat(jnp.arange(n_rows, dtype=F32), GRID_W)
    col = jnp.tile(jnp.arange(GRID_W, dtype=F32), n_rows)
    inv = ROPE_THETA ** (-jnp.arange(ROPE_PAIRS_AXIS, dtype=F32) / ROPE_PAIRS_AXIS)
    ang = jnp.concatenate([row[:, None] * inv, col[:, None] * inv], axis=-1)
    cos = jnp.repeat(jnp.cos(ang), 2, axis=1)
    sin = jnp.repeat(jnp.sin(ang), 2, axis=1) * jnp.tile(jnp.array([-1.0, 1.0], F32), HEAD_DIM // 2)
    pad = ((0, 0), (0, HP - HEAD_DIM))
    return jnp.pad(cos, pad), jnp.pad(sin, pad)


def _pad_heads(w, n_heads):
    L = w.shape[0]
    w = w.reshape(L, D_MODEL, n_heads, HEAD_DIM)
    w = jnp.pad(w, ((0, 0), (0, 0), (0, 0), (0, HP - HEAD_DIM)))
    return w.reshape(L, D_MODEL, n_heads * HP)


def _prep_weights(w_in, a_ln_g, a_ln_b, a_ws, a_bs, b_wg, b_scale, c_q_g, c_k_g, d_conv,
                  p_a, p_b, p_c, p_d, w_o, ln1_g, ln1_b, r_w, r_b, e_w1, e_w3, e_w2, ln2_g, ln2_b):
    L = w_in.shape[0]
    sec_a = w_in[:, :, OFF_A:OFF_B]
    sec_b = w_in[:, :, OFF_B:OFF_C]
    sec_q = _pad_heads(w_in[:, :, OFF_C:OFF_C + C_WIDTH], N_HEADS)
    sec_k = _pad_heads(w_in[:, :, OFF_C + C_WIDTH:OFF_C + C_WIDTH + KV_WIDTH], N_KV_HEADS)
    sec_v = _pad_heads(w_in[:, :, OFF_C + C_WIDTH + KV_WIDTH:OFF_D], N_KV_HEADS)
    sec_d = w_in[:, :, OFF_D:OFF_G]
    sec_g = w_in[:, :, OFF_G:]
    w_in_p = jnp.concatenate([sec_g, sec_a, sec_q, sec_d, sec_b, sec_k, sec_v], axis=-1).astype(BF16)
    row = lambda a: a.reshape(L, 1, -1)
    head_gain = lambda g: jnp.pad(g, ((0, 0), (0, HP - HEAD_DIM))).reshape(L, 1, HP)
    return dict(
        w_in_p=w_in_p, a_ln_g=row(a_ln_g), a_ln_b=row(a_ln_b), a_ws=a_ws.astype(BF16),
        a_bsT=jnp.swapaxes(a_bs, 1, 2), b_wg=b_wg.astype(BF16), b_scale=row(b_scale),
        q_g=head_gain(c_q_g), k_g=head_gain(c_k_g), d_conv=d_conv,
        p_a=p_a.astype(BF16), p_b=p_b.astype(BF16), p_c=p_c.astype(BF16), p_d=p_d.astype(BF16),
        w_o=w_o.astype(BF16), ln1_g=row(ln1_g), ln1_b=row(ln1_b),
        r_wT=jnp.swapaxes(r_w, 1, 2).astype(BF16), r_b=r_b.reshape(L, N_EXPERTS, 1),
        e_w1=e_w1.astype(BF16), e_w3=e_w3.astype(BF16), e_w2=e_w2.astype(BF16),
        ln2_g=row(ln2_g), ln2_b=row(ln2_b))


def _trunk(cfg, x, mod_all, p):
    cos_t, sin_t = _rope_tables(cfg.T)
    for l in range(DEPTH):
        mod = mod_all[l]
        ya, zb, zp, bg, qp, kT, vp, gates = _in_proj(
            cfg, l, x, mod, p["w_in_p"], p["a_ln_g"], p["a_ln_b"], p["a_ws"], p["a_bsT"],
            p["q_g"], p["k_g"], cos_t, sin_t)
        yc = _attention(cfg, qp, kT, vp)
        x1, h2, aff3 = _merge(cfg, l, ya, yc, zb, zp, bg, gates, x, mod, p["b_wg"], p["b_scale"], p["d_conv"],
                              p["p_a"], p["p_b"], p["p_c"], p["p_d"], p["w_o"], p["ln1_g"], p["ln1_b"],
                              p["r_wT"], p["r_b"])
        tauP, meta = _select(cfg, aff3)
        xg = _dispatch(cfg, meta, aff3, tauP, h2.reshape(cfg.n_tok, D_MODEL))
        og = _experts(cfg, l, meta, xg, p["e_w1"], p["e_w3"], p["e_w2"])
        x = _combine(cfg, l, meta, aff3, tauP, x1, mod, p["ln2_g"], p["ln2_b"], og)
    return x


def kernel(x_prompt, x_sample, c_prompt, c_sample, w_ada, b_ada, w_in, a_ln_g, a_ln_b, a_ws, a_bs, b_wg, b_scale, c_q_g, c_k_g, d_conv, p_a, p_b, p_c, p_d, w_o, ln1_g, ln1_b, r_w, r_b, e_w1, e_w3, e_w2, ln2_g, ln2_b):
    p = _prep_weights(w_in, a_ln_g, a_ln_b, a_ws, a_bs, b_wg, b_scale, c_q_g, c_k_g, d_conv,
                      p_a, p_b, p_c, p_d, w_o, ln1_g, ln1_b, r_w, r_b, e_w1, e_w3, e_w2, ln2_g, ln2_b)
    Bp, Bs = x_prompt.shape[0], x_sample.shape[0]
    rows = -(-(Bp + Bs) // SUBLANES) * SUBLANES
    c_all = jnp.pad(jnp.concatenate([c_prompt, c_sample], axis=0), ((0, rows - Bp - Bs), (0, 0)))
    mod = _modulation(c_all, w_ada, b_ada)
    mod_p = mod[:, :Bp, None, :]
    mod_s = mod[:, Bp:Bp + Bs, None, :]
    y_prompt = _trunk(make_cfg(*x_prompt.shape[:2]), x_prompt, mod_p, p)
    y_sample = _trunk(make_cfg(*x_sample.shape[:2]), x_sample, mod_s, p)
    return (y_prompt, y_sample)
```

```python
import functools
from typing import NamedTuple

import numpy as np
import jax
import jax.numpy as jnp
from jax import lax
from jax.experimental import pallas as pl
from jax.experimental.pallas import tpu as pltpu

F32 = jnp.float32
BF16 = jnp.bfloat16
I32 = jnp.int32

D_MODEL = 1024
DEPTH = 4
GRID_W = 64
CHUNK = 128
A_GROUPS = 4
A_WIDTH = 512
POOL_WINDOWS = (2, 4, 8, 16)
B_WIDTH = 512
N_HEADS = 8
N_KV_HEADS = 2
HEAD_DIM = 64
C_WIDTH = N_HEADS * HEAD_DIM
KV_WIDTH = N_KV_HEADS * HEAD_DIM
ROPE_THETA = 10000.0
ROPE_PAIRS_AXIS = HEAD_DIM // 4
D_WIDTH = 512
CONV_W = 3
N_BRANCHES = 4
N_EXPERTS = 16
D_FF_EXPERT = 1024
EC_CAPACITY = 2
ALPHA = (2 * DEPTH) ** 0.25
LN_EPS = 1e-5
RMS_EPS = 1e-6

OFF_A = 0
OFF_B = OFF_A + 2 * A_WIDTH
OFF_C = OFF_B + B_WIDTH
OFF_D = OFF_C + C_WIDTH + 2 * KV_WIDTH
OFF_G = OFF_D + 3 * D_WIDTH
N_IN = OFF_G + N_BRANCHES * D_MODEL

LANES = 128
SUBLANES = 8
BF16_ROWS = 16
VMEM_LIMIT = 56 * 1024 * 1024

MAX_S_BOUND = 40.0
HP = LANES
HALO = SUBLANES


class Cfg(NamedTuple):
    B: int
    T: int
    TT: int
    TQ: int
    TKC: int
    W: int
    TM: int

    @property
    def n_tok(self): return self.B * self.T
    @property
    def nT(self): return self.T // self.TT
    @property
    def NT(self): return self.n_tok // self.TT
    @property
    def NTP(self): return -(-(self.NT + 1) // LANES) * LANES
    @property
    def cap(self): return EC_CAPACITY * self.n_tok // N_EXPERTS
    @property
    def capP(self):
        worst = self.cap + (BF16_ROWS - 1) * self.NT + self.W
        return -(-worst // self.TM) * self.TM


def make_cfg(B, T):
    return Cfg(B=B, T=T, TT=256, TQ=256, TKC=512, W=64, TM=256)


def _cparams(sem):
    return pltpu.CompilerParams(dimension_semantics=sem, vmem_limit_bytes=VMEM_LIMIT)


def _ln(x):
    mu = jnp.mean(x, axis=-1, keepdims=True)
    xc = x - mu
    var = jnp.mean(xc * xc, axis=-1, keepdims=True)
    return xc * lax.rsqrt(var + LN_EPS)


def _gelu_tanh(x):
    return 0.5 * x * (1.0 + jnp.tanh(np.sqrt(2.0 / np.pi).astype(np.float32) * (x + 0.044715 * (x * x * x))))


def _sigmoid(x):
    return 1.0 / (1.0 + jnp.exp(-x))


def _dot(a, b):
    return jnp.dot(a, b, preferred_element_type=F32)


def _once(block, index_map):
    return pl.BlockSpec(block, index_map, pipeline_mode=pl.Buffered(1))


def _mod_body(c_ref, w_ref, b_ref, o_ref):
    c = c_ref[...]
    s = (c * _sigmoid(c)).astype(BF16)
    o_ref[0] = _dot(s, w_ref[0].astype(BF16)) + b_ref[0]


def _modulation(c_all, w_ada, b_ada):
    R = c_all.shape[0]
    L = w_ada.shape[0]
    nb = 1536
    return pl.pallas_call(
        _mod_body,
        grid=(L, 6 * D_MODEL // nb),
        in_specs=[pl.BlockSpec((R, D_MODEL), lambda l, j: (0, 0)),
                  pl.BlockSpec((1, D_MODEL, nb), lambda l, j: (l, 0, j)),
                  pl.BlockSpec((1, 1, nb), lambda l, j: (l, 0, j))],
        out_specs=pl.BlockSpec((1, R, nb), lambda l, j: (l, 0, j)),
        out_shape=jax.ShapeDtypeStruct((L, R, 6 * D_MODEL), F32),
        compiler_params=_cparams(("arbitrary", "arbitrary")),
        name="modulation",
    )(c_all, w_ada, b_ada.reshape(L, 1, 6 * D_MODEL))


def _rms_rope(z, g, cosb, sinb, even):
    ms = jnp.sum(z * z, axis=-1, keepdims=True) * (1.0 / HEAD_DIM)
    zn = z * lax.rsqrt(ms + RMS_EPS) * g
    partner = jnp.where(even, pltpu.roll(zn, HP - 1, 1), pltpu.roll(zn, 1, 1))
    return zn * cosb + partner * sinb


def _in_proj_body(x_ref, mod_ref, wa_ref, wq_ref, wg_ref, wd_ref, wb_ref, wk_ref, wv_ref,
                  alng_ref, alnb_ref, aws_ref, absT_ref, qg_ref, kg_ref, koff_ref, cos_ref, sin_ref,
                  ya_ref, zb_ref, zp_ref, bg_ref, qp_ref, kT_ref, vp_ref, gates_ref, *, TT):
    x = x_ref[0]
    mod = mod_ref[0]
    sh1 = mod[:, 0:D_MODEL]
    sc1 = mod[:, D_MODEL:2 * D_MODEL]
    h = (_ln(x) * (1.0 + sc1) + sh1).astype(BF16)

    ga = _gelu_tanh(_dot(h, wa_ref[0]))
    u = ga[:, :A_WIDTH]
    vn = (_ln(ga[:, A_WIDTH:]) * alng_ref[0] + alnb_ref[0]).astype(BF16)
    bsT = absT_ref[0]
    gc = A_WIDTH // A_GROUPS
    for n in range(TT // CHUNK):
        rows = slice(n * CHUNK, (n + 1) * CHUNK)
        for g in range(A_GROUPS):
            cols = slice(g * gc, (g + 1) * gc)
            sv = _dot(aws_ref[0, g], vn[rows, cols]) + bsT[:, g:g + 1]
            ya_ref[0, rows, cols] = (u[rows, cols] * sv).astype(BF16)

    cosb = cos_ref[...]
    sinb = sin_ref[...]
    lane = lax.broadcasted_iota(I32, (TT, HP), 1)
    even = (lane & 1) == 0
    zq = _dot(h, wq_ref[0])
    for hh in range(N_HEADS):
        q = _rms_rope(zq[:, hh * HP:(hh + 1) * HP], qg_ref[0], cosb, sinb, even)
        q = jnp.where(lane == HEAD_DIM, 1.0, q * (HEAD_DIM ** -0.5))
        qp_ref[0, :, hh * HP:(hh + 1) * HP] = q.astype(BF16)
    zk = _dot(h, wk_ref[0])
    zv = _dot(h, wv_ref[0])
    for kv in range(N_KV_HEADS):
        k = _rms_rope(zk[:, kv * HP:(kv + 1) * HP], kg_ref[0], cosb, sinb, even) + koff_ref[0]
        kT_ref[0, kv] = k.T.astype(BF16)
        v = jnp.where(lane == HEAD_DIM, 1.0, zv[:, kv * HP:(kv + 1) * HP])
        vp_ref[0, kv] = v.astype(BF16)

    zd = _dot(h, wd_ref[0])
    zp_ref[0] = zd[:, 2 * D_WIDTH:] * zd[:, :D_WIDTH]
    bg_ref[0] = zd[:, D_WIDTH:2 * D_WIDTH]
    zb_ref[0] = _dot(h, wb_ref[0])
    for c in range(N_BRANCHES):
        cols = slice(c * D_MODEL, (c + 1) * D_MODEL)
        gates_ref[0, :, cols] = _sigmoid(_dot(h, wg_ref[0, :, cols]))


def _in_proj(cfg, l, x, mod, w_in_p, a_ln_g, a_ln_b, a_ws, a_bsT, q_g, k_g, k_off, cos_t, sin_t):
    B, T, TT, nT = cfg.B, cfg.T, cfg.TT, cfg.nT
    tok = lambda b, i: (b, i, 0)
    lay3 = lambda b, i: (l, 0, 0)
    wspec = lambda width, start: _once((1, D_MODEL, width), lambda b, i: (l, 0, start // width))
    in_specs = [
        pl.BlockSpec((1, TT, D_MODEL), tok),
        pl.BlockSpec((1, 1, 6 * D_MODEL), lambda b, i: (b, 0, 0)),
        wspec(1024, 4096), wspec(1024, 5120), wspec(4096, 0), wspec(1536, 6144),
        wspec(512, 7680), wspec(256, 8192), wspec(256, 8448),
        _once((1, 1, A_WIDTH), lay3), _once((1, 1, A_WIDTH), lay3),
        _once((1, A_GROUPS, CHUNK, CHUNK), lambda b, i: (l, 0, 0, 0)),
        _once((1, CHUNK, A_GROUPS), lay3),
        _once((1, 1, HP), lay3), _once((1, 1, HP), lay3), _once((1, 1, HP), lay3),
        pl.BlockSpec((TT, HP), lambda b, i: (i, 0)), pl.BlockSpec((TT, HP), lambda b, i: (i, 0)),
    ]
    out_shape = [
        jax.ShapeDtypeStruct((B, T, A_WIDTH), BF16),
        jax.ShapeDtypeStruct((B, T, B_WIDTH), F32),
        jax.ShapeDtypeStruct((B, T, D_WIDTH), F32),
        jax.ShapeDtypeStruct((B, T, D_WIDTH), F32),
        jax.ShapeDtypeStruct((B, T, N_HEADS * HP), BF16),
        jax.ShapeDtypeStruct((B, N_KV_HEADS, HP, T), BF16),
        jax.ShapeDtypeStruct((B, N_KV_HEADS, T, HP), BF16),
        jax.ShapeDtypeStruct((B, T, N_BRANCHES * D_MODEL), F32),
    ]
    out_specs = [
        pl.BlockSpec((1, TT, A_WIDTH), tok), pl.BlockSpec((1, TT, B_WIDTH), tok),
        pl.BlockSpec((1, TT, D_WIDTH), tok), pl.BlockSpec((1, TT, D_WIDTH), tok),
        pl.BlockSpec((1, TT, N_HEADS * HP), tok),
        pl.BlockSpec((1, N_KV_HEADS, HP, TT), lambda b, i: (b, 0, 0, i)),
        pl.BlockSpec((1, N_KV_HEADS, TT, HP), lambda b, i: (b, 0, i, 0)),
        pl.BlockSpec((1, TT, N_BRANCHES * D_MODEL), tok),
    ]
    return pl.pallas_call(
        functools.partial(_in_proj_body, TT=TT),
        grid=(B, nT), in_specs=in_specs, out_specs=out_specs, out_shape=out_shape,
        compiler_params=_cparams(("parallel", "parallel")),
        name="in_proj",
    )(x, mod, w_in_p, w_in_p, w_in_p, w_in_p, w_in_p, w_in_p, w_in_p,
      a_ln_g, a_ln_b, a_ws, a_bsT, q_g, k_g, k_off, cos_t, sin_t)


def _attn_finish(acc_sc, o_ref, TQ):
    lane = lax.broadcasted_iota(I32, (TQ, HP), 1)
    for pp in range(N_HEADS // 2):
        acc_even = acc_sc[2 * pp]
        acc_odd = acc_sc[2 * pp + 1]
        o_even = acc_even / acc_even[:, HEAD_DIM:HEAD_DIM + 1]
        o_odd = acc_odd / acc_odd[:, HEAD_DIM:HEAD_DIM + 1]
        pair = jnp.where(lane < HEAD_DIM, o_even, pltpu.roll(o_odd, HEAD_DIM, 1))
        o_ref[0, :, pp * HP:(pp + 1) * HP] = pair.astype(BF16)


def _attn_bounded_body(q_ref, kT_ref, v_ref, o_ref, acc_sc, *, TQ, TKC, T):
    G = N_HEADS // N_KV_HEADS
    acc_sc[...] = jnp.zeros(acc_sc.shape, F32)

    def step(j, carry):
        off = pl.multiple_of(j * TKC, TKC)
        scores = lambda h: _dot(q_ref[0, :, h * HP:(h + 1) * HP], kT_ref[0, h // G, :, pl.ds(off, TKC)])
        s_next = scores(0)
        for h in range(N_HEADS):
            s = s_next
            if h + 1 < N_HEADS:
                s_next = scores(h + 1)
            acc_sc[h] += _dot(jnp.exp(s).astype(BF16), v_ref[0, h // G, pl.ds(off, TKC), :])
        return carry

    lax.fori_loop(0, T // TKC, step, 0)
    _attn_finish(acc_sc, o_ref, TQ)


def _attn_body(q_ref, kT_ref, v_ref, o_ref, m_sc, acc_sc, *, TQ, TKC, T):
    G = N_HEADS // N_KV_HEADS
    m_sc[...] = jnp.full(m_sc.shape, -jnp.inf, F32)
    acc_sc[...] = jnp.zeros(acc_sc.shape, F32)

    def step(j, carry):
        off = pl.multiple_of(j * TKC, TKC)
        for kv in range(N_KV_HEADS):
            kt = kT_ref[0, kv, :, pl.ds(off, TKC)]
            vv = v_ref[0, kv, pl.ds(off, TKC), :]
            for g in range(G):
                h = kv * G + g
                s = _dot(q_ref[0, :, h * HP:(h + 1) * HP], kt)
                m_prev = m_sc[h]
                m_new = jnp.maximum(m_prev, jnp.max(s, axis=1, keepdims=True))
                p = jnp.exp(s - m_new).astype(BF16)
                acc_sc[h] = jnp.exp(m_prev - m_new) * acc_sc[h] + _dot(p, vv)
                m_sc[h] = m_new
        return carry

    lax.fori_loop(0, T // TKC, step, 0)
    _attn_finish(acc_sc, o_ref, TQ)


def _attention(cfg, qp, kT, vp, bounded):
    B, T, TQ, TKC = cfg.B, cfg.T, cfg.TQ, cfg.TKC
    acc = pltpu.VMEM((N_HEADS, TQ, HP), F32)
    if bounded:
        body, scratch, name = _attn_bounded_body, [acc], "attention_bounded"
    else:
        body, scratch, name = _attn_body, [pltpu.VMEM((N_HEADS, TQ, 1), F32), acc], "attention"
    return pl.pallas_call(
        functools.partial(body, TQ=TQ, TKC=TKC, T=T),
        grid=(B, T // TQ),
        in_specs=[pl.BlockSpec((1, TQ, N_HEADS * HP), lambda b, i: (b, i, 0)),
                  pl.BlockSpec((1, N_KV_HEADS, HP, T), lambda b, i: (b, 0, 0, 0)),
                  pl.BlockSpec((1, N_KV_HEADS, T, HP), lambda b, i: (b, 0, 0, 0))],
        out_specs=pl.BlockSpec((1, TQ, C_WIDTH), lambda b, i: (b, i, 0)),
        out_shape=jax.ShapeDtypeStruct((B, T, C_WIDTH), BF16),
        scratch_shapes=scratch,
        compiler_params=_cparams(("parallel", "arbitrary")),
        name=name,
    )(qp, kT, vp)


def _merge_body(ya_ref, yc_ref, zb_ref, zb_prev_ref, zb_next_ref, zp_ref, zp_prev_ref, zp_next_ref,
                bg_ref, gates_ref, x_ref, mod_ref, bwg_ref, bscale_ref, dconv_ref,
                pa_ref, pb_ref, pc_ref, pd_ref, wo_ref, ln1g_ref, ln1b_ref, rwT_ref, rb_ref,
                x1_ref, h2_ref, aff_ref, *, TT, T):
    i = pl.program_id(1)
    first = i == 0
    last = i == pl.num_programs(1) - 1
    R = TT + 2 * HALO

    def with_halo(cur_ref, prev_ref, next_ref):
        prev = jnp.where(first, 0.0, prev_ref[0])
        nxt = jnp.where(last, 0.0, next_ref[0])
        return jnp.concatenate([prev, cur_ref[0], nxt], axis=0)

    up = lambda a, k: pltpu.roll(a, k, 0)
    down = lambda a, k: pltpu.roll(a, R - k, 0)

    zbe = with_halo(zb_ref, zb_prev_ref, zb_next_ref)
    t = i * TT + lax.broadcasted_iota(I32, (TT, 1), 0)
    gc = B_WIDTH // len(POOL_WINDOWS)
    mixed = []
    for g, w in enumerate(POOL_WINDOWS):
        e = zbe[:, g * gc:(g + 1) * gc]
        s = up(e, 1) + e
        half = 1
        while 2 * half < w:
            s = up(s, half) + down(s, half)
            half *= 2
        cnt = (jnp.minimum(t + w // 2, T) - jnp.maximum(t - w // 2, 0)).astype(F32)
        pooled = s[HALO:HALO + TT] / cnt - e[HALO:HALO + TT]
        mixed.append(_dot(pooled.astype(BF16), bwg_ref[0, g]))
    yb = jnp.concatenate(mixed, axis=1) * bscale_ref[0]

    zpe = with_halo(zp_ref, zp_prev_ref, zp_next_ref)
    cw = dconv_ref[0]
    conv = up(zpe, 1) * cw[0:1] + zpe * cw[1:2] + down(zpe, 1) * cw[2:3]
    yd = bg_ref[0] * conv[HALO:HALO + TT]

    merged = gates_ref[0, :, 0:D_MODEL] * _dot(ya_ref[0], pa_ref[0])
    merged += gates_ref[0, :, D_MODEL:2 * D_MODEL] * _dot(yb.astype(BF16), pb_ref[0])
    merged += gates_ref[0, :, 2 * D_MODEL:3 * D_MODEL] * _dot(yc_ref[0], pc_ref[0])
    merged += gates_ref[0, :, 3 * D_MODEL:4 * D_MODEL] * _dot(yd.astype(BF16), pd_ref[0])
    y = _dot(merged.astype(BF16), wo_ref[0])

    mod = mod_ref[0]
    g1 = mod[:, 2 * D_MODEL:3 * D_MODEL]
    sh2 = mod[:, 3 * D_MODEL:4 * D_MODEL]
    sc2 = mod[:, 4 * D_MODEL:5 * D_MODEL]
    x1 = _ln(ALPHA * x_ref[0] + g1 * y) * ln1g_ref[0] + ln1b_ref[0]
    x1_ref[0] = x1

    h2 = (_ln(x1) * (1.0 + sc2) + sh2).astype(BF16)
    h2_ref[0] = h2
    logits = lax.dot_general(rwT_ref[0], h2, (((1,), (1,)), ((), ())), preferred_element_type=F32) + rb_ref[0]
    ex = jnp.exp(logits - jnp.max(logits, axis=0, keepdims=True))
    aff_ref[0] = ex / jnp.sum(ex, axis=0, keepdims=True)


def _merge(cfg, l, ya, yc, zb, zp, bg, gates, x, mod, b_wg, b_scale, d_conv, p_a, p_b, p_c, p_d, w_o,
           ln1_g, ln1_b, r_wT, r_b):
    B, T, TT, nT = cfg.B, cfg.T, cfg.TT, cfg.nT
    hb = TT // HALO
    tok = lambda b, i: (b, i, 0)
    prev = lambda b, i: (b, jnp.maximum(i * hb - 1, 0), 0)
    nxt = lambda b, i: (b, jnp.minimum((i + 1) * hb, T // HALO - 1), 0)
    lay3 = lambda b, i: (l, 0, 0)
    in_specs = [
        pl.BlockSpec((1, TT, A_WIDTH), tok), pl.BlockSpec((1, TT, C_WIDTH), tok),
        pl.BlockSpec((1, TT, B_WIDTH), tok), pl.BlockSpec((1, HALO, B_WIDTH), prev), pl.BlockSpec((1, HALO, B_WIDTH), nxt),
        pl.BlockSpec((1, TT, D_WIDTH), tok), pl.BlockSpec((1, HALO, D_WIDTH), prev), pl.BlockSpec((1, HALO, D_WIDTH), nxt),
        pl.BlockSpec((1, TT, D_WIDTH), tok),
        pl.BlockSpec((1, TT, N_BRANCHES * D_MODEL), tok),
        pl.BlockSpec((1, TT, D_MODEL), tok),
        pl.BlockSpec((1, 1, 6 * D_MODEL), lambda b, i: (b, 0, 0)),
        _once((1, len(POOL_WINDOWS), LANES, LANES), lambda b, i: (l, 0, 0, 0)),
        _once((1, 1, B_WIDTH), lay3), _once((1, CONV_W, D_WIDTH), lay3),
        _once((1, A_WIDTH, D_MODEL), lay3), _once((1, B_WIDTH, D_MODEL), lay3),
        _once((1, C_WIDTH, D_MODEL), lay3), _once((1, D_WIDTH, D_MODEL), lay3),
        _once((1, D_MODEL, D_MODEL), lay3),
        _once((1, 1, D_MODEL), lay3), _once((1, 1, D_MODEL), lay3),
        _once((1, N_EXPERTS, D_MODEL), lay3), _once((1, N_EXPERTS, 1), lay3),
    ]
    out_shape = [jax.ShapeDtypeStruct((B, T, D_MODEL), F32),
                 jax.ShapeDtypeStruct((B, T, D_MODEL), BF16),
                 jax.ShapeDtypeStruct((cfg.NT, N_EXPERTS, TT), F32)]
    out_specs = [pl.BlockSpec((1, TT, D_MODEL), tok), pl.BlockSpec((1, TT, D_MODEL), tok),
                 pl.BlockSpec((1, N_EXPERTS, TT), lambda b, i: (b * nT + i, 0, 0))]
    return pl.pallas_call(
        functools.partial(_merge_body, TT=TT, T=T),
        grid=(B, nT), in_specs=in_specs, out_specs=out_specs, out_shape=out_shape,
        compiler_params=_cparams(("parallel", "arbitrary")),
        name="merge",
    )(ya, yc, zb, zb, zb, zp, zp, zp, bg, gates, x, mod, b_wg, b_scale, d_conv,
      p_a, p_b, p_c, p_d, w_o, ln1_g, ln1_b, r_wT, r_b)


def _selected(aff_tile, tau, tie_end, tile, TT):
    bits = pltpu.bitcast(aff_tile, I32)
    tok = tile * TT + lax.broadcasted_iota(I32, bits.shape, 1)
    return (bits > tau) | ((bits == tau) & (tok < tie_end))


def _count(mask):
    return jnp.sum(jnp.sum(jnp.where(mask, 1.0, 0.0), axis=0), axis=1, keepdims=True)


def _select_body(aff_ref, tauP_ref, meta_ref, *, NT, TT, NTP, cap):
    bits = pltpu.bitcast(aff_ref[...], I32)
    capf = float(cap)

    def tau_step(k, prefix):
        cand = prefix | jnp.left_shift(jnp.int32(1), 30 - k)
        return jnp.where(_count(bits >= cand[None]) >= capf, cand, prefix)
    tau = lax.fori_loop(0, 31, tau_step, jnp.zeros((N_EXPERTS, 1), I32))

    need = capf - _count(bits > tau[None])
    tie = bits == tau[None]
    tok = (lax.broadcasted_iota(I32, bits.shape, 0) * TT + lax.broadcasted_iota(I32, bits.shape, 2))
    n_bits = int(NT * TT).bit_length()

    def tie_step(k, end):
        cand = end | jnp.left_shift(jnp.int32(1), n_bits - 1 - k)
        return jnp.where(_count(tie & (tok < cand[None])) <= need, cand, end)
    tie_end = lax.fori_loop(0, n_bits, tie_step, jnp.zeros((N_EXPERTS, 1), I32))

    lane = lax.broadcasted_iota(I32, (N_EXPERTS, LANES), 1)
    tauP_ref[...] = jnp.where(lane == 0, tau, jnp.where(lane == 1, tie_end, 0))

    col = lax.broadcasted_iota(I32, (N_EXPERTS, NTP), 1)

    def cnt_step(i, acc):
        sel = _selected(aff_ref[i], tau, tie_end, i, TT)
        c = jnp.sum(jnp.where(sel, 1.0, 0.0), axis=1, keepdims=True)
        return jnp.where(col == i, c, acc)
    counts = lax.fori_loop(0, NT, cnt_step, jnp.zeros((N_EXPERTS, NTP), F32)).astype(I32)
    padded = ((counts + (BF16_ROWS - 1)) // BF16_ROWS) * BF16_ROWS
    before = (lax.broadcasted_iota(I32, (NTP, NTP), 0) < lax.broadcasted_iota(I32, (NTP, NTP), 1))
    offs = _dot(padded.astype(BF16), jnp.where(before, 1.0, 0.0).astype(BF16))
    meta_ref[0:N_EXPERTS, :] = offs.astype(I32)
    meta_ref[N_EXPERTS:2 * N_EXPERTS, :] = counts


def _select(cfg, aff3):
    NT, TT, NTP = cfg.NT, cfg.TT, cfg.NTP
    return pl.pallas_call(
        functools.partial(_select_body, NT=NT, TT=TT, NTP=NTP, cap=cfg.cap),
        grid=(1,),
        in_specs=[pl.BlockSpec((NT, N_EXPERTS, TT), lambda i: (0, 0, 0))],
        out_specs=[pl.BlockSpec((N_EXPERTS, LANES), lambda i: (0, 0)),
                   pl.BlockSpec((2 * N_EXPERTS, NTP), lambda i: (0, 0))],
        out_shape=[jax.ShapeDtypeStruct((N_EXPERTS, LANES), I32),
                   jax.ShapeDtypeStruct((2 * N_EXPERTS, NTP), I32)],
        compiler_params=_cparams(("arbitrary",)),
        name="select",
    )(aff3)


def _tile_onehots(aff_ref, tauP_ref, tile, r, TT, W):
    tau = tauP_ref[:, 0:1]
    tie_end = tauP_ref[:, 1:2]
    sel = _selected(aff_ref[0], tau, tie_end, tile, TT)
    before = (lax.broadcasted_iota(I32, (TT, TT), 0) < lax.broadcasted_iota(I32, (TT, TT), 1))
    rank = _dot(jnp.where(sel, 1.0, 0.0).astype(BF16), jnp.where(before, 1.0, 0.0).astype(BF16))
    slot = (r * W + lax.broadcasted_iota(I32, (W, TT), 0)).astype(F32)
    hots = []
    for e in range(N_EXPERTS):
        hit = sel[e:e + 1, :] & (rank[e:e + 1, :] == slot)
        hots.append(jnp.where(hit, 1.0, 0.0))
    return jnp.concatenate(hots, axis=0)


def _rounds(meta_ref, tile, W):
    n_max = meta_ref[N_EXPERTS, tile]
    for e in range(1, N_EXPERTS):
        n_max = jnp.maximum(n_max, meta_ref[N_EXPERTS + e, tile])
    return (n_max + (W - 1)) // W


def _window_live(meta_ref, e, tile, r, W):
    n = meta_ref[N_EXPERTS + e, tile]
    padded = ((n + (BF16_ROWS - 1)) // BF16_ROWS) * BF16_ROWS
    return r * W < padded


def _window_start(meta_ref, e, tile, r, W, capP):
    return pl.multiple_of(e * capP + meta_ref[e, tile] + r * W, BF16_ROWS)


def _dispatch_body(meta_ref, aff_ref, tauP_ref, h_ref, xg_in_ref, xg_ref, stage, sem, *, TT, W, capP):
    del xg_in_ref
    i = pl.program_id(0)
    slot = i % 2

    def copies(tile, r, buf):
        out = []
        for e in range(N_EXPERTS):
            cp = pltpu.make_async_copy(stage.at[buf, pl.ds(e * W, W)],
                                       xg_ref.at[pl.ds(_window_start(meta_ref, e, tile, r, W, capP), W)],
                                       sem.at[buf])
            out.append((_window_live(meta_ref, e, tile, r, W), cp))
        return out

    def start(tile, r, buf):
        for live, cp in copies(tile, r, buf):
            @pl.when(live)
            def _():
                cp.start()

    def wait(tile, r, buf):
        for live, cp in copies(tile, r, buf):
            @pl.when(live)
            def _():
                cp.wait()

    nr = _rounds(meta_ref, i, W)
    nr_prev = _rounds(meta_ref, jnp.maximum(i - 1, 0), W)
    prev_pending = (i > 0) & (nr_prev > 0)

    def one_round(r, carry):
        @pl.when(r > 0)
        def _():
            wait(i, r - 1, slot)
        hot = _tile_onehots(aff_ref, tauP_ref, i, r, TT, W).astype(BF16)
        stage[slot] = _dot(hot, h_ref[...]).astype(BF16)

        @pl.when((r == 0) & prev_pending)
        def _():
            wait(i - 1, nr_prev - 1, 1 - slot)
        start(i, r, slot)
        return carry

    lax.fori_loop(0, nr, one_round, 0)

    @pl.when((nr == 0) & prev_pending)
    def _():
        wait(i - 1, nr_prev - 1, 1 - slot)

    @pl.when((i == pl.num_programs(0) - 1) & (nr > 0))
    def _():
        wait(i, nr - 1, slot)


def _dispatch(cfg, meta, aff3, tauP, h2):
    NT, TT, W, capP = cfg.NT, cfg.TT, cfg.W, cfg.capP
    xg0 = jnp.zeros((N_EXPERTS * capP, D_MODEL), BF16)
    grid_spec = pltpu.PrefetchScalarGridSpec(
        num_scalar_prefetch=1, grid=(NT,),
        in_specs=[pl.BlockSpec((1, N_EXPERTS, TT), lambda i, m: (i, 0, 0)),
                  pl.BlockSpec((N_EXPERTS, LANES), lambda i, m: (0, 0)),
                  pl.BlockSpec((TT, D_MODEL), lambda i, m: (i, 0)),
                  pl.BlockSpec(memory_space=pl.ANY)],
        out_specs=pl.BlockSpec(memory_space=pl.ANY),
        scratch_shapes=[pltpu.VMEM((2, N_EXPERTS * W, D_MODEL), BF16), pltpu.SemaphoreType.DMA((2,))])
    return pl.pallas_call(
        functools.partial(_dispatch_body, TT=TT, W=W, capP=capP),
        grid_spec=grid_spec,
        out_shape=jax.ShapeDtypeStruct((N_EXPERTS * capP, D_MODEL), BF16),
        input_output_aliases={4: 0},
        compiler_params=_cparams(("arbitrary",)),
        name="dispatch",
    )(meta, aff3, tauP, h2, xg0)


def _experts_body(meta_ref, xg_ref, w1_ref, w3_ref, w2_ref, o_ref, *, NT, TM, W):
    e = pl.program_id(0)
    k = pl.program_id(1)
    used = meta_ref[e, NT] + W

    @pl.when(k * TM < used)
    def _():
        x = xg_ref[...]
        a = _dot(x, w1_ref[0, 0])
        b = _dot(x, w3_ref[0, 0])
        o_ref[...] = _dot((a * _sigmoid(a) * b).astype(BF16), w2_ref[0, 0])

    @pl.when(k * TM >= used)
    def _():
        o_ref[...] = jnp.zeros(o_ref.shape, F32)


def _experts(cfg, l, meta, xg, w1, w3, w2):
    TM, capP = cfg.TM, cfg.capP
    nk = capP // TM
    wspec = pl.BlockSpec((1, 1, D_MODEL, D_FF_EXPERT), lambda e, k, m: (l, e, 0, 0))
    grid_spec = pltpu.PrefetchScalarGridSpec(
        num_scalar_prefetch=1, grid=(N_EXPERTS, nk),
        in_specs=[pl.BlockSpec((TM, D_MODEL), lambda e, k, m: (e * nk + k, 0)), wspec, wspec,
                  pl.BlockSpec((1, 1, D_FF_EXPERT, D_MODEL), lambda e, k, m: (l, e, 0, 0))],
        out_specs=pl.BlockSpec((TM, D_MODEL), lambda e, k, m: (e * nk + k, 0)))
    return pl.pallas_call(
        functools.partial(_experts_body, NT=cfg.NT, TM=TM, W=cfg.W),
        grid_spec=grid_spec,
        out_shape=jax.ShapeDtypeStruct((N_EXPERTS * capP, D_MODEL), F32),
        compiler_params=_cparams(("arbitrary", "arbitrary")),
        name="experts",
    )(meta, xg, w1, w3, w2)


def _combine_body(meta_ref, aff_ref, tauP_ref, x1_ref, mod_ref, ln2g_ref, ln2b_ref, og_ref,
                  x2_ref, rows, acc, sem, *, TT, W, capP):
    i = pl.program_id(0)

    def copies(r):
        out = []
        for e in range(N_EXPERTS):
            cp = pltpu.make_async_copy(og_ref.at[pl.ds(_window_start(meta_ref, e, i, r, W, capP), W)],
                                       rows.at[pl.ds(e * W, W)], sem.at[0])
            out.append((_window_live(meta_ref, e, i, r, W), cp))
        return out

    acc[...] = jnp.zeros(acc.shape, F32)
    aff = aff_ref[0]

    def one_round(r, carry):
        for e, (live, cp) in enumerate(copies(r)):
            @pl.when(live)
            def _():
                cp.start()

            @pl.when(jnp.logical_not(live))
            def _():
                rows[pl.ds(e * W, W), :] = jnp.zeros((W, D_MODEL), F32)
        hot = _tile_onehots(aff_ref, tauP_ref, i, r, TT, W)
        gate = jnp.concatenate(
            [jnp.sum(hot[e * W:(e + 1) * W] * aff[e:e + 1, :], axis=1, keepdims=True) for e in range(N_EXPERTS)], axis=0)
        for live, cp in copies(r):
            @pl.when(live)
            def _():
                cp.wait()
        scaled = rows[...] * gate
        hi = scaled.astype(BF16)
        lo = (scaled - hi.astype(F32)).astype(BF16)
        hot16 = hot.astype(BF16)
        tn = (((0,), (0,)), ((), ()))
        acc[...] += (lax.dot_general(hot16, hi, tn, preferred_element_type=F32)
                     + lax.dot_general(hot16, lo, tn, preferred_element_type=F32))
        return carry

    lax.fori_loop(0, _rounds(meta_ref, i, W), one_round, 0)
    g2 = mod_ref[0][:, 5 * D_MODEL:6 * D_MODEL]
    x2_ref[0] = _ln(ALPHA * x1_ref[0] + g2 * acc[...]) * ln2g_ref[0] + ln2b_ref[0]


def _combine(cfg, l, meta, aff3, tauP, x1, mod, ln2_g, ln2_b, og):
    B, T, TT, nT, NT, W, capP = cfg.B, cfg.T, cfg.TT, cfg.nT, cfg.NT, cfg.W, cfg.capP
    tok = lambda i, m: (i // nT, i % nT, 0)
    lay3 = lambda i, m: (l, 0, 0)
    grid_spec = pltpu.PrefetchScalarGridSpec(
        num_scalar_prefetch=1, grid=(NT,),
        in_specs=[pl.BlockSpec((1, N_EXPERTS, TT), lambda i, m: (i, 0, 0)),
                  pl.BlockSpec((N_EXPERTS, LANES), lambda i, m: (0, 0)),
                  pl.BlockSpec((1, TT, D_MODEL), tok),
                  pl.BlockSpec((1, 1, 6 * D_MODEL), lambda i, m: (i // nT, 0, 0)),
                  pl.BlockSpec((1, 1, D_MODEL), lay3), pl.BlockSpec((1, 1, D_MODEL), lay3),
                  pl.BlockSpec(memory_space=pl.ANY)],
        out_specs=pl.BlockSpec((1, TT, D_MODEL), tok),
        scratch_shapes=[pltpu.VMEM((N_EXPERTS * W, D_MODEL), F32), pltpu.VMEM((TT, D_MODEL), F32),
                        pltpu.SemaphoreType.DMA((1,))])
    return pl.pallas_call(
        functools.partial(_combine_body, TT=TT, W=W, capP=capP),
        grid_spec=grid_spec,
        out_shape=jax.ShapeDtypeStruct((B, T, D_MODEL), F32),
        compiler_params=_cparams(("arbitrary",)),
        name="combine",
    )(meta, aff3, tauP, x1, mod, ln2_g, ln2_b, og)


def _rope_tables(T):
    n_rows = T // GRID_W
    row = jnp.repeat(jnp.arange(n_rows, dtype=F32), GRID_W)
    col = jnp.tile(jnp.arange(GRID_W, dtype=F32), n_rows)
    inv = ROPE_THETA ** (-jnp.arange(ROPE_PAIRS_AXIS, dtype=F32) / ROPE_PAIRS_AXIS)
    ang = jnp.concatenate([row[:, None] * inv, col[:, None] * inv], axis=-1)
    cos = jnp.repeat(jnp.cos(ang), 2, axis=1)
    sin = jnp.repeat(jnp.sin(ang), 2, axis=1) * jnp.tile(jnp.array([-1.0, 1.0], F32), HEAD_DIM // 2)
    pad = ((0, 0), (0, HP - HEAD_DIM))
    return jnp.pad(cos, pad), jnp.pad(sin, pad)


def _pad_heads(w, n_heads):
    L = w.shape[0]
    w = w.reshape(L, D_MODEL, n_heads, HEAD_DIM)
    w = jnp.pad(w, ((0, 0), (0, 0), (0, 0), (0, HP - HEAD_DIM)))
    return w.reshape(L, D_MODEL, n_heads * HP)


def _prep_weights(w_in, a_ln_g, a_ln_b, a_ws, a_bs, b_wg, b_scale, c_q_g, c_k_g, d_conv,
                  p_a, p_b, p_c, p_d, w_o, ln1_g, ln1_b, r_w, r_b, e_w1, e_w3, e_w2, ln2_g, ln2_b):
    L = w_in.shape[0]
    sec_a = w_in[:, :, OFF_A:OFF_B]
    sec_b = w_in[:, :, OFF_B:OFF_C]
    sec_q = _pad_heads(w_in[:, :, OFF_C:OFF_C + C_WIDTH], N_HEADS)
    sec_k = _pad_heads(w_in[:, :, OFF_C + C_WIDTH:OFF_C + C_WIDTH + KV_WIDTH], N_KV_HEADS)
    sec_v = _pad_heads(w_in[:, :, OFF_C + C_WIDTH + KV_WIDTH:OFF_D], N_KV_HEADS)
    sec_d = w_in[:, :, OFF_D:OFF_G]
    sec_g = w_in[:, :, OFF_G:]
    w_in_p = jnp.concatenate([sec_g, sec_a, sec_q, sec_d, sec_b, sec_k, sec_v], axis=-1).astype(BF16)
    row = lambda a: a.reshape(L, 1, -1)
    head_gain = lambda g: jnp.pad(g, ((0, 0), (0, HP - HEAD_DIM))).reshape(L, 1, HP)
    s_bound = 1.01 * (HEAD_DIM ** 0.5) * jnp.max(jnp.abs(c_q_g), axis=1) * jnp.max(jnp.abs(c_k_g), axis=1)
    k_off = jnp.zeros((L, 1, HP), F32).at[:, 0, HEAD_DIM].set(-s_bound)
    return dict(
        s_bound=s_bound, k_off=k_off,
        w_in_p=w_in_p, a_ln_g=row(a_ln_g), a_ln_b=row(a_ln_b), a_ws=a_ws.astype(BF16),
        a_bsT=jnp.swapaxes(a_bs, 1, 2), b_wg=b_wg.astype(BF16), b_scale=row(b_scale),
        q_g=head_gain(c_q_g), k_g=head_gain(c_k_g), d_conv=d_conv,
        p_a=p_a.astype(BF16), p_b=p_b.astype(BF16), p_c=p_c.astype(BF16), p_d=p_d.astype(BF16),
        w_o=w_o.astype(BF16), ln1_g=row(ln1_g), ln1_b=row(ln1_b),
        r_wT=jnp.swapaxes(r_w, 1, 2).astype(BF16), r_b=r_b.reshape(L, N_EXPERTS, 1),
        e_w1=e_w1.astype(BF16), e_w3=e_w3.astype(BF16), e_w2=e_w2.astype(BF16),
        ln2_g=row(ln2_g), ln2_b=row(ln2_b))


def _trunk(cfg, x, mod_all, p):
    cos_t, sin_t = _rope_tables(cfg.T)
    for l in range(DEPTH):
        mod = mod_all[l]
        ya, zb, zp, bg, qp, kT, vp, gates = _in_proj(
            cfg, l, x, mod, p["w_in_p"], p["a_ln_g"], p["a_ln_b"], p["a_ws"], p["a_bsT"],
            p["q_g"], p["k_g"], p["k_off"], cos_t, sin_t)
        yc = lax.cond(p["s_bound"][l] <= MAX_S_BOUND,
                      functools.partial(_attention, cfg, bounded=True),
                      functools.partial(_attention, cfg, bounded=False), qp, kT, vp)
        x1, h2, aff3 = _merge(cfg, l, ya, yc, zb, zp, bg, gates, x, mod, p["b_wg"], p["b_scale"], p["d_conv"],
                              p["p_a"], p["p_b"], p["p_c"], p["p_d"], p["w_o"], p["ln1_g"], p["ln1_b"],
                              p["r_wT"], p["r_b"])
        tauP, meta = _select(cfg, aff3)
        xg = _dispatch(cfg, meta, aff3, tauP, h2.reshape(cfg.n_tok, D_MODEL))
        og = _experts(cfg, l, meta, xg, p["e_w1"], p["e_w3"], p["e_w2"])
        x = _combine(cfg, l, meta, aff3, tauP, x1, mod, p["ln2_g"], p["ln2_b"], og)
    return x


def kernel(x_prompt, x_sample, c_prompt, c_sample, w_ada, b_ada, w_in, a_ln_g, a_ln_b, a_ws, a_bs, b_wg, b_scale, c_q_g, c_k_g, d_conv, p_a, p_b, p_c, p_d, w_o, ln1_g, ln1_b, r_w, r_b, e_w1, e_w3, e_w2, ln2_g, ln2_b):
    p = _prep_weights(w_in, a_ln_g, a_ln_b, a_ws, a_bs, b_wg, b_scale, c_q_g, c_k_g, d_conv,
                      p_a, p_b, p_c, p_d, w_o, ln1_g, ln1_b, r_w, r_b, e_w1, e_w3, e_w2, ln2_g, ln2_b)
    Bp, Bs = x_prompt.shape[0], x_sample.shape[0]
    rows = -(-(Bp + Bs) // SUBLANES) * SUBLANES
    c_all = jnp.pad(jnp.concatenate([c_prompt, c_sample], axis=0), ((0, rows - Bp - Bs), (0, 0)))
    mod = _modulation(c_all, w_ada, b_ada)
    mod_p = mod[:, :Bp, None, :]
    mod_s = mod[:, Bp:Bp + Bs, None, :]
    y_prompt = _trunk(make_cfg(*x_prompt.shape[:2]), x_prompt, mod_p, p)
    y_sample = _trunk(make_cfg(*x_sample.shape[:2]), x_sample, mod_s, p)
    return (y_prompt, y_sample)
```

```python
import functools
from typing import NamedTuple

import numpy as np
import jax
import jax.numpy as jnp
from jax import lax
from jax.experimental import pallas as pl
from jax.experimental.pallas import tpu as pltpu

F32 = jnp.float32
BF16 = jnp.bfloat16
I32 = jnp.int32

D_MODEL = 1024
DEPTH = 4
GRID_W = 64
CHUNK = 128
A_GROUPS = 4
A_WIDTH = 512
POOL_WINDOWS = (2, 4, 8, 16)
B_WIDTH = 512
N_HEADS = 8
N_KV_HEADS = 2
HEAD_DIM = 64
C_WIDTH = N_HEADS * HEAD_DIM
KV_WIDTH = N_KV_HEADS * HEAD_DIM
ROPE_THETA = 10000.0
ROPE_PAIRS_AXIS = HEAD_DIM // 4
D_WIDTH = 512
CONV_W = 3
N_BRANCHES = 4
N_EXPERTS = 16
D_FF_EXPERT = 1024
EC_CAPACITY = 2
ALPHA = (2 * DEPTH) ** 0.25
LN_EPS = 1e-5
RMS_EPS = 1e-6

OFF_A = 0
OFF_B = OFF_A + 2 * A_WIDTH
OFF_C = OFF_B + B_WIDTH
OFF_D = OFF_C + C_WIDTH + 2 * KV_WIDTH
OFF_G = OFF_D + 3 * D_WIDTH
N_IN = OFF_G + N_BRANCHES * D_MODEL

LANES = 128
SUBLANES = 8
BF16_ROWS = 16
VMEM_LIMIT = 56 * 1024 * 1024

MAX_S_BOUND = 40.0
HP = LANES
HALO = SUBLANES


class Cfg(NamedTuple):
    B: int
    T: int
    TT: int
    TQ: int
    TKC: int
    W: int
    TM: int

    @property
    def n_tok(self): return self.B * self.T
    @property
    def nT(self): return self.T // self.TT
    @property
    def NT(self): return self.n_tok // self.TT
    @property
    def NTP(self): return -(-(self.NT + 1) // LANES) * LANES
    @property
    def cap(self): return EC_CAPACITY * self.n_tok // N_EXPERTS
    @property
    def capP(self):
        worst = self.cap + (BF16_ROWS - 1) * self.NT + self.W
        return -(-worst // self.TM) * self.TM


def make_cfg(B, T):
    return Cfg(B=B, T=T, TT=256, TQ=256, TKC=1024, W=48, TM=256)


def _cparams(sem):
    return pltpu.CompilerParams(dimension_semantics=sem, vmem_limit_bytes=VMEM_LIMIT)


def _ln(x):
    mu = jnp.mean(x, axis=-1, keepdims=True)
    xc = x - mu
    var = jnp.mean(xc * xc, axis=-1, keepdims=True)
    return xc * lax.rsqrt(var + LN_EPS)


def _gelu_tanh(x):
    return 0.5 * x * (1.0 + jnp.tanh(np.sqrt(2.0 / np.pi).astype(np.float32) * (x + 0.044715 * (x * x * x))))


def _sigmoid(x):
    return 1.0 / (1.0 + jnp.exp(-x))


def _dot(a, b):
    return jnp.dot(a, b, preferred_element_type=F32)


def _once(block, index_map):
    return pl.BlockSpec(block, index_map, pipeline_mode=pl.Buffered(1))


def _mod_body(c_ref, w_ref, b_ref, o_ref):
    c = c_ref[...]
    s = (c * _sigmoid(c)).astype(BF16)
    o_ref[0] = _dot(s, w_ref[0].astype(BF16)) + b_ref[0]


def _modulation(c_all, w_ada, b_ada):
    R = c_all.shape[0]
    L = w_ada.shape[0]
    nb = 1536
    return pl.pallas_call(
        _mod_body,
        grid=(L, 6 * D_MODEL // nb),
        in_specs=[pl.BlockSpec((R, D_MODEL), lambda l, j: (0, 0)),
                  pl.BlockSpec((1, D_MODEL, nb), lambda l, j: (l, 0, j)),
                  pl.BlockSpec((1, 1, nb), lambda l, j: (l, 0, j))],
        out_specs=pl.BlockSpec((1, R, nb), lambda l, j: (l, 0, j)),
        out_shape=jax.ShapeDtypeStruct((L, R, 6 * D_MODEL), F32),
        compiler_params=_cparams(("arbitrary", "arbitrary")),
        name="modulation",
    )(c_all, w_ada, b_ada.reshape(L, 1, 6 * D_MODEL))


def _rms_rope(z, g, cosb, sinb, even):
    ms = jnp.sum(z * z, axis=-1, keepdims=True) * (1.0 / HEAD_DIM)
    zn = z * lax.rsqrt(ms + RMS_EPS) * g
    partner = jnp.where(even, pltpu.roll(zn, HP - 1, 1), pltpu.roll(zn, 1, 1))
    return zn * cosb + partner * sinb


def _in_proj_body(x_ref, mod_ref, wa_ref, wq_ref, wg_ref, wd_ref, wb_ref, wk_ref, wv_ref,
                  alng_ref, alnb_ref, aws_ref, absT_ref, qg_ref, kg_ref, koff_ref, cos_ref, sin_ref,
                  ya_ref, zb_ref, zp_ref, bg_ref, qp_ref, kT_ref, vp_ref, gates_ref, *, TT):
    x = x_ref[0]
    mod = mod_ref[0]
    sh1 = mod[:, 0:D_MODEL]
    sc1 = mod[:, D_MODEL:2 * D_MODEL]
    h = (_ln(x) * (1.0 + sc1) + sh1).astype(BF16)

    ga = _gelu_tanh(_dot(h, wa_ref[0]))
    u = ga[:, :A_WIDTH]
    vn = (_ln(ga[:, A_WIDTH:]) * alng_ref[0] + alnb_ref[0]).astype(BF16)
    bsT = absT_ref[0]
    gc = A_WIDTH // A_GROUPS
    for n in range(TT // CHUNK):
        rows = slice(n * CHUNK, (n + 1) * CHUNK)
        for g in range(A_GROUPS):
            cols = slice(g * gc, (g + 1) * gc)
            sv = _dot(aws_ref[0, g], vn[rows, cols]) + bsT[:, g:g + 1]
            ya_ref[0, rows, cols] = (u[rows, cols] * sv).astype(BF16)

    cosb = cos_ref[...]
    sinb = sin_ref[...]
    lane = lax.broadcasted_iota(I32, (TT, HP), 1)
    even = (lane & 1) == 0
    zq = _dot(h, wq_ref[0])
    for hh in range(N_HEADS):
        q = _rms_rope(zq[:, hh * HP:(hh + 1) * HP], qg_ref[0], cosb, sinb, even)
        q = jnp.where(lane == HEAD_DIM, 1.0, q * (HEAD_DIM ** -0.5))
        qp_ref[0, :, hh * HP:(hh + 1) * HP] = q.astype(BF16)
    zk = _dot(h, wk_ref[0])
    zv = _dot(h, wv_ref[0])
    for kv in range(N_KV_HEADS):
        k = _rms_rope(zk[:, kv * HP:(kv + 1) * HP], kg_ref[0], cosb, sinb, even) + koff_ref[0]
        kT_ref[0, kv] = k.T.astype(BF16)
        v = jnp.where(lane == HEAD_DIM, 1.0, zv[:, kv * HP:(kv + 1) * HP])
        vp_ref[0, kv] = v.astype(BF16)

    zd = _dot(h, wd_ref[0])
    zp_ref[0] = zd[:, 2 * D_WIDTH:] * zd[:, :D_WIDTH]
    bg_ref[0] = zd[:, D_WIDTH:2 * D_WIDTH]
    zb_ref[0] = _dot(h, wb_ref[0])
    for c in range(N_BRANCHES):
        cols = slice(c * D_MODEL, (c + 1) * D_MODEL)
        gates_ref[0, :, cols] = _sigmoid(_dot(h, wg_ref[0, :, cols]))


def _in_proj(cfg, l, x, mod, w_in_p, a_ln_g, a_ln_b, a_ws, a_bsT, q_g, k_g, k_off, cos_t, sin_t):
    B, T, TT, nT = cfg.B, cfg.T, cfg.TT, cfg.nT
    tok = lambda b, i: (b, i, 0)
    lay3 = lambda b, i: (l, 0, 0)
    wspec = lambda width, start: _once((1, D_MODEL, width), lambda b, i: (l, 0, start // width))
    in_specs = [
        pl.BlockSpec((1, TT, D_MODEL), tok),
        pl.BlockSpec((1, 1, 6 * D_MODEL), lambda b, i: (b, 0, 0)),
        wspec(1024, 4096), wspec(1024, 5120), wspec(4096, 0), wspec(1536, 6144),
        wspec(512, 7680), wspec(256, 8192), wspec(256, 8448),
        _once((1, 1, A_WIDTH), lay3), _once((1, 1, A_WIDTH), lay3),
        _once((1, A_GROUPS, CHUNK, CHUNK), lambda b, i: (l, 0, 0, 0)),
        _once((1, CHUNK, A_GROUPS), lay3),
        _once((1, 1, HP), lay3), _once((1, 1, HP), lay3), _once((1, 1, HP), lay3),
        pl.BlockSpec((TT, HP), lambda b, i: (i, 0)), pl.BlockSpec((TT, HP), lambda b, i: (i, 0)),
    ]
    out_shape = [
        jax.ShapeDtypeStruct((B, T, A_WIDTH), BF16),
        jax.ShapeDtypeStruct((B, T, B_WIDTH), F32),
        jax.ShapeDtypeStruct((B, T, D_WIDTH), F32),
        jax.ShapeDtypeStruct((B, T, D_WIDTH), F32),
        jax.ShapeDtypeStruct((B, T, N_HEADS * HP), BF16),
        jax.ShapeDtypeStruct((B, N_KV_HEADS, HP, T), BF16),
        jax.ShapeDtypeStruct((B, N_KV_HEADS, T, HP), BF16),
        jax.ShapeDtypeStruct((B, T, N_BRANCHES * D_MODEL), F32),
    ]
    out_specs = [
        pl.BlockSpec((1, TT, A_WIDTH), tok), pl.BlockSpec((1, TT, B_WIDTH), tok),
        pl.BlockSpec((1, TT, D_WIDTH), tok), pl.BlockSpec((1, TT, D_WIDTH), tok),
        pl.BlockSpec((1, TT, N_HEADS * HP), tok),
        pl.BlockSpec((1, N_KV_HEADS, HP, TT), lambda b, i: (b, 0, 0, i)),
        pl.BlockSpec((1, N_KV_HEADS, TT, HP), lambda b, i: (b, 0, i, 0)),
        pl.BlockSpec((1, TT, N_BRANCHES * D_MODEL), tok),
    ]
    return pl.pallas_call(
        functools.partial(_in_proj_body, TT=TT),
        grid=(B, nT), in_specs=in_specs, out_specs=out_specs, out_shape=out_shape,
        compiler_params=_cparams(("parallel", "parallel")),
        name="in_proj",
    )(x, mod, w_in_p, w_in_p, w_in_p, w_in_p, w_in_p, w_in_p, w_in_p,
      a_ln_g, a_ln_b, a_ws, a_bsT, q_g, k_g, k_off, cos_t, sin_t)


def _attn_finish(acc_sc, o_ref, TQ):
    lane = lax.broadcasted_iota(I32, (TQ, HP), 1)
    for pp in range(N_HEADS // 2):
        acc_even = acc_sc[2 * pp]
        acc_odd = acc_sc[2 * pp + 1]
        o_even = acc_even / acc_even[:, HEAD_DIM:HEAD_DIM + 1]
        o_odd = acc_odd / acc_odd[:, HEAD_DIM:HEAD_DIM + 1]
        pair = jnp.where(lane < HEAD_DIM, o_even, pltpu.roll(o_odd, HEAD_DIM, 1))
        o_ref[0, :, pp * HP:(pp + 1) * HP] = pair.astype(BF16)


def _attn_bounded_body(q_ref, kT_ref, v_ref, o_ref, acc_sc, *, TQ, TKC, T):
    G = N_HEADS // N_KV_HEADS
    acc_sc[...] = jnp.zeros(acc_sc.shape, F32)

    def step(j, carry):
        off = pl.multiple_of(j * TKC, TKC)
        scores = lambda h: _dot(q_ref[0, :, h * HP:(h + 1) * HP], kT_ref[0, h // G, :, pl.ds(off, TKC)])
        s_next = scores(0)
        for h in range(N_HEADS):
            s = s_next
            if h + 1 < N_HEADS:
                s_next = scores(h + 1)
            acc_sc[h] += _dot(jnp.exp(s).astype(BF16), v_ref[0, h // G, pl.ds(off, TKC), :])
        return carry

    lax.fori_loop(0, T // TKC, step, 0)
    _attn_finish(acc_sc, o_ref, TQ)


def _attn_body(q_ref, kT_ref, v_ref, o_ref, m_sc, acc_sc, *, TQ, TKC, T):
    G = N_HEADS // N_KV_HEADS
    m_sc[...] = jnp.full(m_sc.shape, -jnp.inf, F32)
    acc_sc[...] = jnp.zeros(acc_sc.shape, F32)

    def step(j, carry):
        off = pl.multiple_of(j * TKC, TKC)
        for kv in range(N_KV_HEADS):
            kt = kT_ref[0, kv, :, pl.ds(off, TKC)]
            vv = v_ref[0, kv, pl.ds(off, TKC), :]
            for g in range(G):
                h = kv * G + g
                s = _dot(q_ref[0, :, h * HP:(h + 1) * HP], kt)
                m_prev = m_sc[h]
                m_new = jnp.maximum(m_prev, jnp.max(s, axis=1, keepdims=True))
                p = jnp.exp(s - m_new).astype(BF16)
                acc_sc[h] = jnp.exp(m_prev - m_new) * acc_sc[h] + _dot(p, vv)
                m_sc[h] = m_new
        return carry

    lax.fori_loop(0, T // TKC, step, 0)
    _attn_finish(acc_sc, o_ref, TQ)


def _attention(cfg, qp, kT, vp, bounded):
    B, T, TQ, TKC = cfg.B, cfg.T, cfg.TQ, cfg.TKC
    acc = pltpu.VMEM((N_HEADS, TQ, HP), F32)
    if bounded:
        body, scratch, name = _attn_bounded_body, [acc], "attention_bounded"
    else:
        body, scratch, name = _attn_body, [pltpu.VMEM((N_HEADS, TQ, 1), F32), acc], "attention"
    return pl.pallas_call(
        functools.partial(body, TQ=TQ, TKC=TKC, T=T),
        grid=(B, T // TQ),
        in_specs=[pl.BlockSpec((1, TQ, N_HEADS * HP), lambda b, i: (b, i, 0)),
                  pl.BlockSpec((1, N_KV_HEADS, HP, T), lambda b, i: (b, 0, 0, 0)),
                  pl.BlockSpec((1, N_KV_HEADS, T, HP), lambda b, i: (b, 0, 0, 0))],
        out_specs=pl.BlockSpec((1, TQ, C_WIDTH), lambda b, i: (b, i, 0)),
        out_shape=jax.ShapeDtypeStruct((B, T, C_WIDTH), BF16),
        scratch_shapes=scratch,
        compiler_params=_cparams(("parallel", "arbitrary")),
        name=name,
    )(qp, kT, vp)


def _merge_body(ya_ref, yc_ref, zb_ref, zb_prev_ref, zb_next_ref, zp_ref, zp_prev_ref, zp_next_ref,
                bg_ref, gates_ref, x_ref, mod_ref, bwg_ref, bscale_ref, dconv_ref,
                pa_ref, pb_ref, pc_ref, pd_ref, wo_ref, ln1g_ref, ln1b_ref, rwT_ref, rb_ref,
                x1_ref, h2_ref, aff_ref, *, TT, T):
    i = pl.program_id(1)
    first = i == 0
    last = i == pl.num_programs(1) - 1
    R = TT + 2 * HALO

    def with_halo(cur_ref, prev_ref, next_ref):
        prev = jnp.where(first, 0.0, prev_ref[0])
        nxt = jnp.where(last, 0.0, next_ref[0])
        return jnp.concatenate([prev, cur_ref[0], nxt], axis=0)

    up = lambda a, k: pltpu.roll(a, k, 0)
    down = lambda a, k: pltpu.roll(a, R - k, 0)

    zbe = with_halo(zb_ref, zb_prev_ref, zb_next_ref)
    t = i * TT + lax.broadcasted_iota(I32, (TT, 1), 0)
    gc = B_WIDTH // len(POOL_WINDOWS)
    mixed = []
    for g, w in enumerate(POOL_WINDOWS):
        e = zbe[:, g * gc:(g + 1) * gc]
        s = up(e, 1) + e
        half = 1
        while 2 * half < w:
            s = up(s, half) + down(s, half)
            half *= 2
        cnt = (jnp.minimum(t + w // 2, T) - jnp.maximum(t - w // 2, 0)).astype(F32)
        pooled = s[HALO:HALO + TT] / cnt - e[HALO:HALO + TT]
        mixed.append(_dot(pooled.astype(BF16), bwg_ref[0, g]))
    yb = jnp.concatenate(mixed, axis=1) * bscale_ref[0]

    zpe = with_halo(zp_ref, zp_prev_ref, zp_next_ref)
    cw = dconv_ref[0]
    conv = up(zpe, 1) * cw[0:1] + zpe * cw[1:2] + down(zpe, 1) * cw[2:3]
    yd = bg_ref[0] * conv[HALO:HALO + TT]

    merged = gates_ref[0, :, 0:D_MODEL] * _dot(ya_ref[0], pa_ref[0])
    merged += gates_ref[0, :, D_MODEL:2 * D_MODEL] * _dot(yb.astype(BF16), pb_ref[0])
    merged += gates_ref[0, :, 2 * D_MODEL:3 * D_MODEL] * _dot(yc_ref[0], pc_ref[0])
    merged += gates_ref[0, :, 3 * D_MODEL:4 * D_MODEL] * _dot(yd.astype(BF16), pd_ref[0])
    y = _dot(merged.astype(BF16), wo_ref[0])

    mod = mod_ref[0]
    g1 = mod[:, 2 * D_MODEL:3 * D_MODEL]
    sh2 = mod[:, 3 * D_MODEL:4 * D_MODEL]
    sc2 = mod[:, 4 * D_MODEL:5 * D_MODEL]
    x1 = _ln(ALPHA * x_ref[0] + g1 * y) * ln1g_ref[0] + ln1b_ref[0]
    x1_ref[0] = x1

    h2 = (_ln(x1) * (1.0 + sc2) + sh2).astype(BF16)
    h2_ref[0] = h2
    logits = lax.dot_general(rwT_ref[0], h2, (((1,), (1,)), ((), ())), preferred_element_type=F32) + rb_ref[0]
    ex = jnp.exp(logits - jnp.max(logits, axis=0, keepdims=True))
    aff_ref[0] = ex / jnp.sum(ex, axis=0, keepdims=True)


def _merge(cfg, l, ya, yc, zb, zp, bg, gates, x, mod, b_wg, b_scale, d_conv, p_a, p_b, p_c, p_d, w_o,
           ln1_g, ln1_b, r_wT, r_b):
    B, T, TT, nT = cfg.B, cfg.T, cfg.TT, cfg.nT
    hb = TT // HALO
    tok = lambda b, i: (b, i, 0)
    prev = lambda b, i: (b, jnp.maximum(i * hb - 1, 0), 0)
    nxt = lambda b, i: (b, jnp.minimum((i + 1) * hb, T // HALO - 1), 0)
    lay3 = lambda b, i: (l, 0, 0)
    in_specs = [
        pl.BlockSpec((1, TT, A_WIDTH), tok), pl.BlockSpec((1, TT, C_WIDTH), tok),
        pl.BlockSpec((1, TT, B_WIDTH), tok), pl.BlockSpec((1, HALO, B_WIDTH), prev), pl.BlockSpec((1, HALO, B_WIDTH), nxt),
        pl.BlockSpec((1, TT, D_WIDTH), tok), pl.BlockSpec((1, HALO, D_WIDTH), prev), pl.BlockSpec((1, HALO, D_WIDTH), nxt),
        pl.BlockSpec((1, TT, D_WIDTH), tok),
        pl.BlockSpec((1, TT, N_BRANCHES * D_MODEL), tok),
        pl.BlockSpec((1, TT, D_MODEL), tok),
        pl.BlockSpec((1, 1, 6 * D_MODEL), lambda b, i: (b, 0, 0)),
        _once((1, len(POOL_WINDOWS), LANES, LANES), lambda b, i: (l, 0, 0, 0)),
        _once((1, 1, B_WIDTH), lay3), _once((1, CONV_W, D_WIDTH), lay3),
        _once((1, A_WIDTH, D_MODEL), lay3), _once((1, B_WIDTH, D_MODEL), lay3),
        _once((1, C_WIDTH, D_MODEL), lay3), _once((1, D_WIDTH, D_MODEL), lay3),
        _once((1, D_MODEL, D_MODEL), lay3),
        _once((1, 1, D_MODEL), lay3), _once((1, 1, D_MODEL), lay3),
        _once((1, N_EXPERTS, D_MODEL), lay3), _once((1, N_EXPERTS, 1), lay3),
    ]
    out_shape = [jax.ShapeDtypeStruct((B, T, D_MODEL), F32),
                 jax.ShapeDtypeStruct((B, T, D_MODEL), BF16),
                 jax.ShapeDtypeStruct((cfg.NT, N_EXPERTS, TT), F32)]
    out_specs = [pl.BlockSpec((1, TT, D_MODEL), tok), pl.BlockSpec((1, TT, D_MODEL), tok),
                 pl.BlockSpec((1, N_EXPERTS, TT), lambda b, i: (b * nT + i, 0, 0))]
    return pl.pallas_call(
        functools.partial(_merge_body, TT=TT, T=T),
        grid=(B, nT), in_specs=in_specs, out_specs=out_specs, out_shape=out_shape,
        compiler_params=_cparams(("parallel", "arbitrary")),
        name="merge",
    )(ya, yc, zb, zb, zb, zp, zp, zp, bg, gates, x, mod, b_wg, b_scale, d_conv,
      p_a, p_b, p_c, p_d, w_o, ln1_g, ln1_b, r_wT, r_b)


def _selected(aff_tile, tau, tie_end, tile, TT):
    bits = pltpu.bitcast(aff_tile, I32)
    tok = tile * TT + lax.broadcasted_iota(I32, bits.shape, 1)
    return (bits > tau) | ((bits == tau) & (tok < tie_end))


def _count(mask):
    return jnp.sum(jnp.sum(jnp.where(mask, 1.0, 0.0), axis=0), axis=1, keepdims=True)


def _select_body(aff_ref, tauP_ref, meta_ref, *, NT, TT, NTP, cap):
    bits = pltpu.bitcast(aff_ref[...], I32)
    capf = float(cap)

    def tau_step(k, prefix):
        cand = prefix | jnp.left_shift(jnp.int32(1), 30 - k)
        return jnp.where(_count(bits >= cand[None]) >= capf, cand, prefix)
    tau = lax.fori_loop(0, 31, tau_step, jnp.zeros((N_EXPERTS, 1), I32))

    need = capf - _count(bits > tau[None])
    tie = bits == tau[None]
    tok = (lax.broadcasted_iota(I32, bits.shape, 0) * TT + lax.broadcasted_iota(I32, bits.shape, 2))
    n_bits = int(NT * TT).bit_length()

    def tie_step(k, end):
        cand = end | jnp.left_shift(jnp.int32(1), n_bits - 1 - k)
        return jnp.where(_count(tie & (tok < cand[None])) <= need, cand, end)
    tie_end = lax.fori_loop(0, n_bits, tie_step, jnp.zeros((N_EXPERTS, 1), I32))

    lane = lax.broadcasted_iota(I32, (N_EXPERTS, LANES), 1)
    tauP_ref[...] = jnp.where(lane == 0, tau, jnp.where(lane == 1, tie_end, 0))

    col = lax.broadcasted_iota(I32, (N_EXPERTS, NTP), 1)

    def cnt_step(i, acc):
        sel = _selected(aff_ref[i], tau, tie_end, i, TT)
        c = jnp.sum(jnp.where(sel, 1.0, 0.0), axis=1, keepdims=True)
        return jnp.where(col == i, c, acc)
    counts = lax.fori_loop(0, NT, cnt_step, jnp.zeros((N_EXPERTS, NTP), F32)).astype(I32)
    padded = ((counts + (BF16_ROWS - 1)) // BF16_ROWS) * BF16_ROWS
    before = (lax.broadcasted_iota(I32, (NTP, NTP), 0) < lax.broadcasted_iota(I32, (NTP, NTP), 1))
    offs = _dot(padded.astype(BF16), jnp.where(before, 1.0, 0.0).astype(BF16))
    meta_ref[0:N_EXPERTS, :] = offs.astype(I32)
    meta_ref[N_EXPERTS:2 * N_EXPERTS, :] = counts


def _select(cfg, aff3):
    NT, TT, NTP = cfg.NT, cfg.TT, cfg.NTP
    return pl.pallas_call(
        functools.partial(_select_body, NT=NT, TT=TT, NTP=NTP, cap=cfg.cap),
        grid=(1,),
        in_specs=[pl.BlockSpec((NT, N_EXPERTS, TT), lambda i: (0, 0, 0))],
        out_specs=[pl.BlockSpec((N_EXPERTS, LANES), lambda i: (0, 0)),
                   pl.BlockSpec((2 * N_EXPERTS, NTP), lambda i: (0, 0))],
        out_shape=[jax.ShapeDtypeStruct((N_EXPERTS, LANES), I32),
                   jax.ShapeDtypeStruct((2 * N_EXPERTS, NTP), I32)],
        compiler_params=_cparams(("arbitrary",)),
        name="select",
    )(aff3)


def _tile_slots(aff_ref, tauP_ref, tile, TT):
    tau = tauP_ref[:, 0:1]
    tie_end = tauP_ref[:, 1:2]
    sel = _selected(aff_ref[0], tau, tie_end, tile, TT)
    before = (lax.broadcasted_iota(I32, (TT, TT), 0) < lax.broadcasted_iota(I32, (TT, TT), 1))
    rank = _dot(jnp.where(sel, 1.0, 0.0).astype(BF16), jnp.where(before, 1.0, 0.0).astype(BF16))
    return jnp.where(sel, rank, -1.0)


def _onehots(slots, r, TT, W):
    want = (r * W + lax.broadcasted_iota(I32, (W, TT), 0)).astype(F32)
    return jnp.concatenate([jnp.where(slots[e:e + 1, :] == want, 1.0, 0.0) for e in range(N_EXPERTS)], axis=0)


def _rounds(meta_ref, tile, W):
    n_max = meta_ref[N_EXPERTS, tile]
    for e in range(1, N_EXPERTS):
        n_max = jnp.maximum(n_max, meta_ref[N_EXPERTS + e, tile])
    return (n_max + (W - 1)) // W


def _window_live(meta_ref, e, tile, r, W):
    n = meta_ref[N_EXPERTS + e, tile]
    padded = ((n + (BF16_ROWS - 1)) // BF16_ROWS) * BF16_ROWS
    return r * W < padded


def _window_start(meta_ref, e, tile, r, W, capP):
    return pl.multiple_of(e * capP + meta_ref[e, tile] + r * W, BF16_ROWS)


def _dispatch_body(meta_ref, aff_ref, tauP_ref, h_ref, xg_in_ref, xg_ref, stage, sem, *, TT, W, capP):
    del xg_in_ref
    i = pl.program_id(0)
    slot = i % 2

    def copies(tile, r, buf):
        out = []
        for e in range(N_EXPERTS):
            cp = pltpu.make_async_copy(stage.at[buf, pl.ds(e * W, W)],
                                       xg_ref.at[pl.ds(_window_start(meta_ref, e, tile, r, W, capP), W)],
                                       sem.at[buf])
            out.append((_window_live(meta_ref, e, tile, r, W), cp))
        return out

    def start(tile, r, buf):
        for live, cp in copies(tile, r, buf):
            @pl.when(live)
            def _():
                cp.start()

    def wait(tile, r, buf):
        for live, cp in copies(tile, r, buf):
            @pl.when(live)
            def _():
                cp.wait()

    nr = _rounds(meta_ref, i, W)
    nr_prev = _rounds(meta_ref, jnp.maximum(i - 1, 0), W)
    prev_pending = (i > 0) & (nr_prev > 0)
    slots = _tile_slots(aff_ref, tauP_ref, i, TT)

    def one_round(r, carry):
        @pl.when(r > 0)
        def _():
            wait(i, r - 1, slot)
        hot = _onehots(slots, r, TT, W).astype(BF16)
        stage[slot] = _dot(hot, h_ref[...]).astype(BF16)

        @pl.when((r == 0) & prev_pending)
        def _():
            wait(i - 1, nr_prev - 1, 1 - slot)
        start(i, r, slot)
        return carry

    lax.fori_loop(0, nr, one_round, 0)

    @pl.when((nr == 0) & prev_pending)
    def _():
        wait(i - 1, nr_prev - 1, 1 - slot)

    @pl.when((i == pl.num_programs(0) - 1) & (nr > 0))
    def _():
        wait(i, nr - 1, slot)


def _dispatch(cfg, meta, aff3, tauP, h2):
    NT, TT, W, capP = cfg.NT, cfg.TT, cfg.W, cfg.capP
    xg0 = jnp.zeros((N_EXPERTS * capP, D_MODEL), BF16)
    grid_spec = pltpu.PrefetchScalarGridSpec(
        num_scalar_prefetch=1, grid=(NT,),
        in_specs=[pl.BlockSpec((1, N_EXPERTS, TT), lambda i, m: (i, 0, 0)),
                  pl.BlockSpec((N_EXPERTS, LANES), lambda i, m: (0, 0)),
                  pl.BlockSpec((TT, D_MODEL), lambda i, m: (i, 0)),
                  pl.BlockSpec(memory_space=pl.ANY)],
        out_specs=pl.BlockSpec(memory_space=pl.ANY),
        scratch_shapes=[pltpu.VMEM((2, N_EXPERTS * W, D_MODEL), BF16), pltpu.SemaphoreType.DMA((2,))])
    return pl.pallas_call(
        functools.partial(_dispatch_body, TT=TT, W=W, capP=capP),
        grid_spec=grid_spec,
        out_shape=jax.ShapeDtypeStruct((N_EXPERTS * capP, D_MODEL), BF16),
        input_output_aliases={4: 0},
        compiler_params=_cparams(("arbitrary",)),
        name="dispatch",
    )(meta, aff3, tauP, h2, xg0)


def _experts_body(meta_ref, xg_ref, w1_ref, w3_ref, w2_ref, o_ref, *, NT, TM, W):
    e = pl.program_id(0)
    k = pl.program_id(1)
    used = meta_ref[e, NT] + W

    @pl.when(k * TM < used)
    def _():
        x = xg_ref[...]
        a = _dot(x, w1_ref[0, 0])
        b = _dot(x, w3_ref[0, 0])
        o_ref[...] = _dot((a * _sigmoid(a) * b).astype(BF16), w2_ref[0, 0])

    @pl.when(k * TM >= used)
    def _():
        o_ref[...] = jnp.zeros(o_ref.shape, F32)


def _experts(cfg, l, meta, xg, w1, w3, w2):
    TM, capP = cfg.TM, cfg.capP
    nk = capP // TM
    wspec = pl.BlockSpec((1, 1, D_MODEL, D_FF_EXPERT), lambda e, k, m: (l, e, 0, 0))
    grid_spec = pltpu.PrefetchScalarGridSpec(
        num_scalar_prefetch=1, grid=(N_EXPERTS, nk),
        in_specs=[pl.BlockSpec((TM, D_MODEL), lambda e, k, m: (e * nk + k, 0)), wspec, wspec,
                  pl.BlockSpec((1, 1, D_FF_EXPERT, D_MODEL), lambda e, k, m: (l, e, 0, 0))],
        out_specs=pl.BlockSpec((TM, D_MODEL), lambda e, k, m: (e * nk + k, 0)))
    return pl.pallas_call(
        functools.partial(_experts_body, NT=cfg.NT, TM=TM, W=cfg.W),
        grid_spec=grid_spec,
        out_shape=jax.ShapeDtypeStruct((N_EXPERTS * capP, D_MODEL), F32),
        compiler_params=_cparams(("arbitrary", "arbitrary")),
        name="experts",
    )(meta, xg, w1, w3, w2)


def _combine_body(meta_ref, aff_ref, tauP_ref, x1_ref, mod_ref, ln2g_ref, ln2b_ref, og_ref,
                  x2_ref, rows, acc, sem, *, TT, W, capP):
    i = pl.program_id(0)
    buf = i % 2

    def copies(tile, r, b):
        out = []
        for e in range(N_EXPERTS):
            cp = pltpu.make_async_copy(og_ref.at[pl.ds(_window_start(meta_ref, e, tile, r, W, capP), W)],
                                       rows.at[b, pl.ds(e * W, W)], sem.at[b])
            out.append((_window_live(meta_ref, e, tile, r, W), cp))
        return out

    def start(tile, r, b):
        for live, cp in copies(tile, r, b):
            @pl.when(live)
            def _():
                cp.start()

    @pl.when(i == 0)
    def _():
        start(0, 0, 0)

    @pl.when(i + 1 < pl.num_programs(0))
    def _():
        start(i + 1, 0, 1 - buf)

    acc[...] = jnp.zeros(acc.shape, F32)
    aff = aff_ref[0]
    slots = _tile_slots(aff_ref, tauP_ref, i, TT)

    def one_round(r, carry):
        @pl.when(r > 0)
        def _():
            start(i, r, buf)
        hot = _onehots(slots, r, TT, W)
        gate = jnp.concatenate(
            [jnp.sum(hot[e * W:(e + 1) * W] * aff[e:e + 1, :], axis=1, keepdims=True) for e in range(N_EXPERTS)], axis=0)
        for e, (live, cp) in enumerate(copies(i, r, buf)):
            @pl.when(live)
            def _():
                cp.wait()

            @pl.when(jnp.logical_not(live))
            def _():
                rows[buf, pl.ds(e * W, W), :] = jnp.zeros((W, D_MODEL), F32)
        scaled = rows[buf] * gate
        hi = scaled.astype(BF16)
        lo = (scaled - hi.astype(F32)).astype(BF16)
        hot16 = hot.astype(BF16)
        tn = (((0,), (0,)), ((), ()))
        acc[...] += (lax.dot_general(hot16, hi, tn, preferred_element_type=F32)
                     + lax.dot_general(hot16, lo, tn, preferred_element_type=F32))
        return carry

    lax.fori_loop(0, _rounds(meta_ref, i, W), one_round, 0)
    g2 = mod_ref[0][:, 5 * D_MODEL:6 * D_MODEL]
    x2_ref[0] = _ln(ALPHA * x1_ref[0] + g2 * acc[...]) * ln2g_ref[0] + ln2b_ref[0]


def _combine(cfg, l, meta, aff3, tauP, x1, mod, ln2_g, ln2_b, og):
    B, T, TT, nT, NT, W, capP = cfg.B, cfg.T, cfg.TT, cfg.nT, cfg.NT, cfg.W, cfg.capP
    tok = lambda i, m: (i // nT, i % nT, 0)
    lay3 = lambda i, m: (l, 0, 0)
    grid_spec = pltpu.PrefetchScalarGridSpec(
        num_scalar_prefetch=1, grid=(NT,),
        in_specs=[pl.BlockSpec((1, N_EXPERTS, TT), lambda i, m: (i, 0, 0)),
                  pl.BlockSpec((N_EXPERTS, LANES), lambda i, m: (0, 0)),
                  pl.BlockSpec((1, TT, D_MODEL), tok),
                  pl.BlockSpec((1, 1, 6 * D_MODEL), lambda i, m: (i // nT, 0, 0)),
                  pl.BlockSpec((1, 1, D_MODEL), lay3), pl.BlockSpec((1, 1, D_MODEL), lay3),
                  pl.BlockSpec(memory_space=pl.ANY)],
        out_specs=pl.BlockSpec((1, TT, D_MODEL), tok),
        scratch_shapes=[pltpu.VMEM((2, N_EXPERTS * W, D_MODEL), F32), pltpu.VMEM((TT, D_MODEL), F32),
                        pltpu.SemaphoreType.DMA((2,))])
    return pl.pallas_call(
        functools.partial(_combine_body, TT=TT, W=W, capP=capP),
        grid_spec=grid_spec,
        out_shape=jax.ShapeDtypeStruct((B, T, D_MODEL), F32),
        compiler_params=_cparams(("arbitrary",)),
        name="combine",
    )(meta, aff3, tauP, x1, mod, ln2_g, ln2_b, og)


def _rope_tables(T):
    n_rows = T // GRID_W
    row = jnp.repeat(jnp.arange(n_rows, dtype=F32), GRID_W)
    col = jnp.tile(jnp.arange(GRID_W, dtype=F32), n_rows)
    inv = ROPE_THETA ** (-jnp.arange(ROPE_PAIRS_AXIS, dtype=F32) / ROPE_PAIRS_AXIS)
    ang = jnp.concatenate([row[:, None] * inv, col[:, None] * inv], axis=-1)
    cos = jnp.repeat(jnp.cos(ang), 2, axis=1)
    sin = jnp.repeat(jnp.sin(ang), 2, axis=1) * jnp.tile(jnp.array([-1.0, 1.0], F32), HEAD_DIM // 2)
    pad = ((0, 0), (0, HP - HEAD_DIM))
    return jnp.pad(cos, pad), jnp.pad(sin, pad)


def _pad_heads(w, n_heads):
    L = w.shape[0]
    w = w.reshape(L, D_MODEL, n_heads, HEAD_DIM)
    w = jnp.pad(w, ((0, 0), (0, 0), (0, 0), (0, HP - HEAD_DIM)))
    return w.reshape(L, D_MODEL, n_heads * HP)


def _prep_weights(w_in, a_ln_g, a_ln_b, a_ws, a_bs, b_wg, b_scale, c_q_g, c_k_g, d_conv,
                  p_a, p_b, p_c, p_d, w_o, ln1_g, ln1_b, r_w, r_b, e_w1, e_w3, e_w2, ln2_g, ln2_b):
    L = w_in.shape[0]
    sec_a = w_in[:, :, OFF_A:OFF_B]
    sec_b = w_in[:, :, OFF_B:OFF_C]
    sec_q = _pad_heads(w_in[:, :, OFF_C:OFF_C + C_WIDTH], N_HEADS)
    sec_k = _pad_heads(w_in[:, :, OFF_C + C_WIDTH:OFF_C + C_WIDTH + KV_WIDTH], N_KV_HEADS)
    sec_v = _pad_heads(w_in[:, :, OFF_C + C_WIDTH + KV_WIDTH:OFF_D], N_KV_HEADS)
    sec_d = w_in[:, :, OFF_D:OFF_G]
    sec_g = w_in[:, :, OFF_G:]
    w_in_p = jnp.concatenate([sec_g, sec_a, sec_q, sec_d, sec_b, sec_k, sec_v], axis=-1).astype(BF16)
    row = lambda a: a.reshape(L, 1, -1)
    head_gain = lambda g: jnp.pad(g, ((0, 0), (0, HP - HEAD_DIM))).reshape(L, 1, HP)
    s_bound = 1.01 * (HEAD_DIM ** 0.5) * jnp.max(jnp.abs(c_q_g), axis=1) * jnp.max(jnp.abs(c_k_g), axis=1)
    k_off = jnp.zeros((L, 1, HP), F32).at[:, 0, HEAD_DIM].set(-s_bound)
    return dict(
        s_bound=s_bound, k_off=k_off,
        w_in_p=w_in_p, a_ln_g=row(a_ln_g), a_ln_b=row(a_ln_b), a_ws=a_ws.astype(BF16),
        a_bsT=jnp.swapaxes(a_bs, 1, 2), b_wg=b_wg.astype(BF16), b_scale=row(b_scale),
        q_g=head_gain(c_q_g), k_g=head_gain(c_k_g), d_conv=d_conv,
        p_a=p_a.astype(BF16), p_b=p_b.astype(BF16), p_c=p_c.astype(BF16), p_d=p_d.astype(BF16),
        w_o=w_o.astype(BF16), ln1_g=row(ln1_g), ln1_b=row(ln1_b),
        r_wT=jnp.swapaxes(r_w, 1, 2).astype(BF16), r_b=r_b.reshape(L, N_EXPERTS, 1),
        e_w1=e_w1.astype(BF16), e_w3=e_w3.astype(BF16), e_w2=e_w2.astype(BF16),
        ln2_g=row(ln2_g), ln2_b=row(ln2_b))


def _trunk(cfg, x, mod_all, p):
    cos_t, sin_t = _rope_tables(cfg.T)
    for l in range(DEPTH):
        mod = mod_all[l]
        ya, zb, zp, bg, qp, kT, vp, gates = _in_proj(
            cfg, l, x, mod, p["w_in_p"], p["a_ln_g"], p["a_ln_b"], p["a_ws"], p["a_bsT"],
            p["q_g"], p["k_g"], p["k_off"], cos_t, sin_t)
        yc = lax.cond(p["s_bound"][l] <= MAX_S_BOUND,
                      functools.partial(_attention, cfg, bounded=True),
                      functools.partial(_attention, cfg, bounded=False), qp, kT, vp)
        x1, h2, aff3 = _merge(cfg, l, ya, yc, zb, zp, bg, gates, x, mod, p["b_wg"], p["b_scale"], p["d_conv"],
                              p["p_a"], p["p_b"], p["p_c"], p["p_d"], p["w_o"], p["ln1_g"], p["ln1_b"],
                              p["r_wT"], p["r_b"])
        tauP, meta = _select(cfg, aff3)
        xg = _dispatch(cfg, meta, aff3, tauP, h2.reshape(cfg.n_tok, D_MODEL))
        og = _experts(cfg, l, meta, xg, p["e_w1"], p["e_w3"], p["e_w2"])
        x = _combine(cfg, l, meta, aff3, tauP, x1, mod, p["ln2_g"], p["ln2_b"], og)
    return x


def kernel(x_prompt, x_sample, c_prompt, c_sample, w_ada, b_ada, w_in, a_ln_g, a_ln_b, a_ws, a_bs, b_wg, b_scale, c_q_g, c_k_g, d_conv, p_a, p_b, p_c, p_d, w_o, ln1_g, ln1_b, r_w, r_b, e_w1, e_w3, e_w2, ln2_g, ln2_b):
    p = _prep_weights(w_in, a_ln_g, a_ln_b, a_ws, a_bs, b_wg, b_scale, c_q_g, c_k_g, d_conv,
                      p_a, p_b, p_c, p_d, w_o, ln1_g, ln1_b, r_w, r_b, e_w1, e_w3, e_w2, ln2_g, ln2_b)
    Bp, Bs = x_prompt.shape[0], x_sample.shape[0]
    rows = -(-(Bp + Bs) // SUBLANES) * SUBLANES
    c_all = jnp.pad(jnp.concatenate([c_prompt, c_sample], axis=0), ((0, rows - Bp - Bs), (0, 0)))
    mod = _modulation(c_all, w_ada, b_ada)
    mod_p = mod[:, :Bp, None, :]
    mod_s = mod[:, Bp:Bp + Bs, None, :]
    y_prompt = _trunk(make_cfg(*x_prompt.shape[:2]), x_prompt, mod_p, p)
    y_sample = _trunk(make_cfg(*x_sample.shape[:2]), x_sample, mod_s, p)
    return (y_prompt, y_sample)
```

```python
import functools
from typing import NamedTuple

import numpy as np
import jax
import jax.numpy as jnp
from jax import lax
from jax.experimental import pallas as pl
from jax.experimental.pallas import tpu as pltpu

F32 = jnp.float32
BF16 = jnp.bfloat16
I32 = jnp.int32

D_MODEL = 1024
DEPTH = 4
GRID_W = 64
CHUNK = 128
A_GROUPS = 4
A_WIDTH = 512
POOL_WINDOWS = (2, 4, 8, 16)
B_WIDTH = 512
N_HEADS = 8
N_KV_HEADS = 2
HEAD_DIM = 64
C_WIDTH = N_HEADS * HEAD_DIM
KV_WIDTH = N_KV_HEADS * HEAD_DIM
ROPE_THETA = 10000.0
ROPE_PAIRS_AXIS = HEAD_DIM // 4
D_WIDTH = 512
CONV_W = 3
N_BRANCHES = 4
N_EXPERTS = 16
D_FF_EXPERT = 1024
EC_CAPACITY = 2
ALPHA = (2 * DEPTH) ** 0.25
LN_EPS = 1e-5
RMS_EPS = 1e-6

OFF_A = 0
OFF_B = OFF_A + 2 * A_WIDTH
OFF_C = OFF_B + B_WIDTH
OFF_D = OFF_C + C_WIDTH + 2 * KV_WIDTH
OFF_G = OFF_D + 3 * D_WIDTH
N_IN = OFF_G + N_BRANCHES * D_MODEL

LANES = 128
SUBLANES = 8
BF16_ROWS = 16
VMEM_LIMIT = 56 * 1024 * 1024

MAX_S_BOUND = 40.0
HP = LANES
HALO = SUBLANES


class Cfg(NamedTuple):
    B: int
    T: int
    TT: int
    TQ: int
    TKC: int
    W: int
    TM: int

    @property
    def n_tok(self): return self.B * self.T
    @property
    def nT(self): return self.T // self.TT
    @property
    def NT(self): return self.n_tok // self.TT
    @property
    def NTP(self): return -(-(self.NT + 1) // LANES) * LANES
    @property
    def cap(self): return EC_CAPACITY * self.n_tok // N_EXPERTS
    @property
    def capP(self):
        worst = self.cap + (BF16_ROWS - 1) * self.NT + self.W
        return -(-worst // self.TM) * self.TM


def make_cfg(B, T):
    return Cfg(B=B, T=T, TT=256, TQ=256, TKC=1024, W=48, TM=256)


def _cparams(sem):
    return pltpu.CompilerParams(dimension_semantics=sem, vmem_limit_bytes=VMEM_LIMIT)


def _ln(x):
    mu = jnp.mean(x, axis=-1, keepdims=True)
    xc = x - mu
    var = jnp.mean(xc * xc, axis=-1, keepdims=True)
    return xc * lax.rsqrt(var + LN_EPS)


def _gelu_tanh(x):
    return 0.5 * x * (1.0 + jnp.tanh(np.sqrt(2.0 / np.pi).astype(np.float32) * (x + 0.044715 * (x * x * x))))


def _sigmoid(x):
    return 1.0 / (1.0 + jnp.exp(-x))


def _dot(a, b):
    return jnp.dot(a, b, preferred_element_type=F32)


def _once(block, index_map):
    return pl.BlockSpec(block, index_map, pipeline_mode=pl.Buffered(1))


def _mod_body(c_ref, w_ref, b_ref, o_ref):
    c = c_ref[...]
    s = (c * _sigmoid(c)).astype(BF16)
    o_ref[0] = _dot(s, w_ref[0].astype(BF16)) + b_ref[0]


def _modulation(c_all, w_ada, b_ada):
    R = c_all.shape[0]
    L = w_ada.shape[0]
    nb = 1536
    return pl.pallas_call(
        _mod_body,
        grid=(L, 6 * D_MODEL // nb),
        in_specs=[pl.BlockSpec((R, D_MODEL), lambda l, j: (0, 0)),
                  pl.BlockSpec((1, D_MODEL, nb), lambda l, j: (l, 0, j)),
                  pl.BlockSpec((1, 1, nb), lambda l, j: (l, 0, j))],
        out_specs=pl.BlockSpec((1, R, nb), lambda l, j: (l, 0, j)),
        out_shape=jax.ShapeDtypeStruct((L, R, 6 * D_MODEL), F32),
        compiler_params=_cparams(("arbitrary", "arbitrary")),
        name="modulation",
    )(c_all, w_ada, b_ada.reshape(L, 1, 6 * D_MODEL))


def _rms_rope(z, g, cosb, sinb, even):
    ms = jnp.sum(z * z, axis=-1, keepdims=True) * (1.0 / HEAD_DIM)
    zn = z * lax.rsqrt(ms + RMS_EPS) * g
    partner = jnp.where(even, pltpu.roll(zn, HP - 1, 1), pltpu.roll(zn, 1, 1))
    return zn * cosb + partner * sinb


def _in_proj_body(x_ref, mod_ref, wa_ref, wq_ref, wg_ref, wd_ref, wb_ref, wk_ref, wv_ref,
                  alng_ref, alnb_ref, aws_ref, absT_ref, qg_ref, kg_ref, koff_ref, cos_ref, sin_ref,
                  ya_ref, zb_ref, zp_ref, bg_ref, qp_ref, kT_ref, vp_ref, gates_ref, *, TT):
    x = x_ref[0]
    mod = mod_ref[0]
    sh1 = mod[:, 0:D_MODEL]
    sc1 = mod[:, D_MODEL:2 * D_MODEL]
    h = (_ln(x) * (1.0 + sc1) + sh1).astype(BF16)

    def gate_cols(c):
        return _dot(h, wg_ref[0, :, c * D_MODEL:(c + 1) * D_MODEL])

    def put_gate(c, z):
        gates_ref[0, :, c * D_MODEL:(c + 1) * D_MODEL] = _sigmoid(z)

    za = _dot(h, wa_ref[0])
    zq = _dot(h, wq_ref[0])

    ga = _gelu_tanh(za)
    u = ga[:, :A_WIDTH]
    vn = (_ln(ga[:, A_WIDTH:]) * alng_ref[0] + alnb_ref[0]).astype(BF16)
    bsT = absT_ref[0]
    gc = A_WIDTH // A_GROUPS
    for n in range(TT // CHUNK):
        rows = slice(n * CHUNK, (n + 1) * CHUNK)
        for g in range(A_GROUPS):
            cols = slice(g * gc, (g + 1) * gc)
            sv = _dot(aws_ref[0, g], vn[rows, cols]) + bsT[:, g:g + 1]
            ya_ref[0, rows, cols] = (u[rows, cols] * sv).astype(BF16)

    cosb = cos_ref[...]
    sinb = sin_ref[...]
    lane = lax.broadcasted_iota(I32, (TT, HP), 1)
    even = (lane & 1) == 0
    zg0 = gate_cols(0)
    for hh in range(N_HEADS):
        q = _rms_rope(zq[:, hh * HP:(hh + 1) * HP], qg_ref[0], cosb, sinb, even)
        q = jnp.where(lane == HEAD_DIM, 1.0, q * (HEAD_DIM ** -0.5))
        qp_ref[0, :, hh * HP:(hh + 1) * HP] = q.astype(BF16)
    zk = _dot(h, wk_ref[0])
    zv = _dot(h, wv_ref[0])
    zg1 = gate_cols(1)
    put_gate(0, zg0)
    for kv in range(N_KV_HEADS):
        k = _rms_rope(zk[:, kv * HP:(kv + 1) * HP], kg_ref[0], cosb, sinb, even) + koff_ref[0]
        kT_ref[0, kv] = k.T.astype(BF16)
        v = jnp.where(lane == HEAD_DIM, 1.0, zv[:, kv * HP:(kv + 1) * HP])
        vp_ref[0, kv] = v.astype(BF16)

    zd = _dot(h, wd_ref[0])
    zg2 = gate_cols(2)
    put_gate(1, zg1)
    zp_ref[0] = zd[:, 2 * D_WIDTH:] * zd[:, :D_WIDTH]
    bg_ref[0] = zd[:, D_WIDTH:2 * D_WIDTH]
    zb = _dot(h, wb_ref[0])
    zg3 = gate_cols(3)
    put_gate(2, zg2)
    zb_ref[0] = zb
    put_gate(3, zg3)


def _in_proj(cfg, l, x, mod, w_in_p, a_ln_g, a_ln_b, a_ws, a_bsT, q_g, k_g, k_off, cos_t, sin_t):
    B, T, TT, nT = cfg.B, cfg.T, cfg.TT, cfg.nT
    tok = lambda b, i: (b, i, 0)
    lay3 = lambda b, i: (l, 0, 0)
    wspec = lambda width, start: _once((1, D_MODEL, width), lambda b, i: (l, 0, start // width))
    in_specs = [
        pl.BlockSpec((1, TT, D_MODEL), tok),
        pl.BlockSpec((1, 1, 6 * D_MODEL), lambda b, i: (b, 0, 0)),
        wspec(1024, 4096), wspec(1024, 5120), wspec(4096, 0), wspec(1536, 6144),
        wspec(512, 7680), wspec(256, 8192), wspec(256, 8448),
        _once((1, 1, A_WIDTH), lay3), _once((1, 1, A_WIDTH), lay3),
        _once((1, A_GROUPS, CHUNK, CHUNK), lambda b, i: (l, 0, 0, 0)),
        _once((1, CHUNK, A_GROUPS), lay3),
        _once((1, 1, HP), lay3), _once((1, 1, HP), lay3), _once((1, 1, HP), lay3),
        pl.BlockSpec((TT, HP), lambda b, i: (i, 0)), pl.BlockSpec((TT, HP), lambda b, i: (i, 0)),
    ]
    out_shape = [
        jax.ShapeDtypeStruct((B, T, A_WIDTH), BF16),
        jax.ShapeDtypeStruct((B, T, B_WIDTH), F32),
        jax.ShapeDtypeStruct((B, T, D_WIDTH), F32),
        jax.ShapeDtypeStruct((B, T, D_WIDTH), F32),
        jax.ShapeDtypeStruct((B, T, N_HEADS * HP), BF16),
        jax.ShapeDtypeStruct((B, N_KV_HEADS, HP, T), BF16),
        jax.ShapeDtypeStruct((B, N_KV_HEADS, T, HP), BF16),
        jax.ShapeDtypeStruct((B, T, N_BRANCHES * D_MODEL), F32),
    ]
    out_specs = [
        pl.BlockSpec((1, TT, A_WIDTH), tok), pl.BlockSpec((1, TT, B_WIDTH), tok),
        pl.BlockSpec((1, TT, D_WIDTH), tok), pl.BlockSpec((1, TT, D_WIDTH), tok),
        pl.BlockSpec((1, TT, N_HEADS * HP), tok),
        pl.BlockSpec((1, N_KV_HEADS, HP, TT), lambda b, i: (b, 0, 0, i)),
        pl.BlockSpec((1, N_KV_HEADS, TT, HP), lambda b, i: (b, 0, i, 0)),
        pl.BlockSpec((1, TT, N_BRANCHES * D_MODEL), tok),
    ]
    return pl.pallas_call(
        functools.partial(_in_proj_body, TT=TT),
        grid=(B, nT), in_specs=in_specs, out_specs=out_specs, out_shape=out_shape,
        compiler_params=_cparams(("parallel", "parallel")),
        name="in_proj",
    )(x, mod, w_in_p, w_in_p, w_in_p, w_in_p, w_in_p, w_in_p, w_in_p,
      a_ln_g, a_ln_b, a_ws, a_bsT, q_g, k_g, k_off, cos_t, sin_t)


def _attn_finish(acc_sc, o_ref, TQ):
    lane = lax.broadcasted_iota(I32, (TQ, HP), 1)
    for pp in range(N_HEADS // 2):
        acc_even = acc_sc[2 * pp]
        acc_odd = acc_sc[2 * pp + 1]
        o_even = acc_even / acc_even[:, HEAD_DIM:HEAD_DIM + 1]
        o_odd = acc_odd / acc_odd[:, HEAD_DIM:HEAD_DIM + 1]
        pair = jnp.where(lane < HEAD_DIM, o_even, pltpu.roll(o_odd, HEAD_DIM, 1))
        o_ref[0, :, pp * HP:(pp + 1) * HP] = pair.astype(BF16)


def _attn_bounded_body(q_ref, kT_ref, v_ref, o_ref, acc_sc, *, TQ, TKC, T):
    G = N_HEADS // N_KV_HEADS
    acc_sc[...] = jnp.zeros(acc_sc.shape, F32)

    def step(j, carry):
        off = pl.multiple_of(j * TKC, TKC)
        scores = lambda h: _dot(q_ref[0, :, h * HP:(h + 1) * HP], kT_ref[0, h // G, :, pl.ds(off, TKC)])
        s_next = scores(0)
        for h in range(N_HEADS):
            s = s_next
            if h + 1 < N_HEADS:
                s_next = scores(h + 1)
            acc_sc[h] += _dot(jnp.exp(s).astype(BF16), v_ref[0, h // G, pl.ds(off, TKC), :])
        return carry

    lax.fori_loop(0, T // TKC, step, 0)
    _attn_finish(acc_sc, o_ref, TQ)


def _attn_body(q_ref, kT_ref, v_ref, o_ref, m_sc, acc_sc, *, TQ, TKC, T):
    G = N_HEADS // N_KV_HEADS
    m_sc[...] = jnp.full(m_sc.shape, -jnp.inf, F32)
    acc_sc[...] = jnp.zeros(acc_sc.shape, F32)

    def step(j, carry):
        off = pl.multiple_of(j * TKC, TKC)
        for kv in range(N_KV_HEADS):
            kt = kT_ref[0, kv, :, pl.ds(off, TKC)]
            vv = v_ref[0, kv, pl.ds(off, TKC), :]
            for g in range(G):
                h = kv * G + g
                s = _dot(q_ref[0, :, h * HP:(h + 1) * HP], kt)
                m_prev = m_sc[h]
                m_new = jnp.maximum(m_prev, jnp.max(s, axis=1, keepdims=True))
                p = jnp.exp(s - m_new).astype(BF16)
                acc_sc[h] = jnp.exp(m_prev - m_new) * acc_sc[h] + _dot(p, vv)
                m_sc[h] = m_new
        return carry

    lax.fori_loop(0, T // TKC, step, 0)
    _attn_finish(acc_sc, o_ref, TQ)


def _attention(cfg, qp, kT, vp, bounded):
    B, T, TQ, TKC = cfg.B, cfg.T, cfg.TQ, cfg.TKC
    acc = pltpu.VMEM((N_HEADS, TQ, HP), F32)
    if bounded:
        body, scratch, name = _attn_bounded_body, [acc], "attention_bounded"
    else:
        body, scratch, name = _attn_body, [pltpu.VMEM((N_HEADS, TQ, 1), F32), acc], "attention"
    return pl.pallas_call(
        functools.partial(body, TQ=TQ, TKC=TKC, T=T),
        grid=(B, T // TQ),
        in_specs=[pl.BlockSpec((1, TQ, N_HEADS * HP), lambda b, i: (b, i, 0)),
                  pl.BlockSpec((1, N_KV_HEADS, HP, T), lambda b, i: (b, 0, 0, 0)),
                  pl.BlockSpec((1, N_KV_HEADS, T, HP), lambda b, i: (b, 0, 0, 0))],
        out_specs=pl.BlockSpec((1, TQ, C_WIDTH), lambda b, i: (b, i, 0)),
        out_shape=jax.ShapeDtypeStruct((B, T, C_WIDTH), BF16),
        scratch_shapes=scratch,
        compiler_params=_cparams(("parallel", "arbitrary")),
        name=name,
    )(qp, kT, vp)


def _merge_body(ya_ref, yc_ref, zb_ref, zb_prev_ref, zb_next_ref, zp_ref, zp_prev_ref, zp_next_ref,
                bg_ref, gates_ref, x_ref, mod_ref, bwg_ref, bscale_ref, dconv_ref,
                pa_ref, pb_ref, pc_ref, pd_ref, wo_ref, ln1g_ref, ln1b_ref, rwT_ref, rb_ref,
                x1_ref, h2_ref, aff_ref, *, TT, T):
    i = pl.program_id(1)
    first = i == 0
    last = i == pl.num_programs(1) - 1
    R = TT + 2 * HALO

    def with_halo(cur_ref, prev_ref, next_ref):
        prev = jnp.where(first, 0.0, prev_ref[0])
        nxt = jnp.where(last, 0.0, next_ref[0])
        return jnp.concatenate([prev, cur_ref[0], nxt], axis=0)

    up = lambda a, k: pltpu.roll(a, k, 0)
    down = lambda a, k: pltpu.roll(a, R - k, 0)

    zbe = with_halo(zb_ref, zb_prev_ref, zb_next_ref)
    t = i * TT + lax.broadcasted_iota(I32, (TT, 1), 0)
    gc = B_WIDTH // len(POOL_WINDOWS)
    mixed = []
    for g, w in enumerate(POOL_WINDOWS):
        e = zbe[:, g * gc:(g + 1) * gc]
        s = up(e, 1) + e
        half = 1
        while 2 * half < w:
            s = up(s, half) + down(s, half)
            half *= 2
        cnt = (jnp.minimum(t + w // 2, T) - jnp.maximum(t - w // 2, 0)).astype(F32)
        pooled = s[HALO:HALO + TT] / cnt - e[HALO:HALO + TT]
        mixed.append(_dot(pooled.astype(BF16), bwg_ref[0, g]))
    yb = jnp.concatenate(mixed, axis=1) * bscale_ref[0]

    zpe = with_halo(zp_ref, zp_prev_ref, zp_next_ref)
    cw = dconv_ref[0]
    conv = up(zpe, 1) * cw[0:1] + zpe * cw[1:2] + down(zpe, 1) * cw[2:3]
    yd = bg_ref[0] * conv[HALO:HALO + TT]

    mod = mod_ref[0]
    g1 = mod[:, 2 * D_MODEL:3 * D_MODEL]
    sh2 = mod[:, 3 * D_MODEL:4 * D_MODEL]
    sc2 = mod[:, 4 * D_MODEL:5 * D_MODEL]
    merged = gates_ref[0, :, 0:D_MODEL] * _dot(ya_ref[0], pa_ref[0])
    merged += gates_ref[0, :, D_MODEL:2 * D_MODEL] * _dot(yb.astype(BF16), pb_ref[0])
    merged += gates_ref[0, :, 2 * D_MODEL:3 * D_MODEL] * _dot(yc_ref[0], pc_ref[0])
    merged += gates_ref[0, :, 3 * D_MODEL:4 * D_MODEL] * _dot(yd.astype(BF16), pd_ref[0])
    y = _dot(merged.astype(BF16), wo_ref[0])
    x1 = _ln(ALPHA * x_ref[0] + g1 * y) * ln1g_ref[0] + ln1b_ref[0]
    x1_ref[0] = x1

    h2 = (_ln(x1) * (1.0 + sc2) + sh2).astype(BF16)
    h2_ref[0] = h2
    logits = lax.dot_general(rwT_ref[0], h2, (((1,), (1,)), ((), ())), preferred_element_type=F32) + rb_ref[0]
    ex = jnp.exp(logits - jnp.max(logits, axis=0, keepdims=True))
    aff_ref[0] = ex / jnp.sum(ex, axis=0, keepdims=True)


def _merge(cfg, l, ya, yc, zb, zp, bg, gates, x, mod, b_wg, b_scale, d_conv, p_a, p_b, p_c, p_d, w_o,
           ln1_g, ln1_b, r_wT, r_b):
    B, T, TT, nT = cfg.B, cfg.T, cfg.TT, cfg.nT
    hb = TT // HALO
    tok = lambda b, i: (b, i, 0)
    prev = lambda b, i: (b, jnp.maximum(i * hb - 1, 0), 0)
    nxt = lambda b, i: (b, jnp.minimum((i + 1) * hb, T // HALO - 1), 0)
    lay3 = lambda b, i: (l, 0, 0)
    in_specs = [
        pl.BlockSpec((1, TT, A_WIDTH), tok), pl.BlockSpec((1, TT, C_WIDTH), tok),
        pl.BlockSpec((1, TT, B_WIDTH), tok), pl.BlockSpec((1, HALO, B_WIDTH), prev), pl.BlockSpec((1, HALO, B_WIDTH), nxt),
        pl.BlockSpec((1, TT, D_WIDTH), tok), pl.BlockSpec((1, HALO, D_WIDTH), prev), pl.BlockSpec((1, HALO, D_WIDTH), nxt),
        pl.BlockSpec((1, TT, D_WIDTH), tok),
        pl.BlockSpec((1, TT, N_BRANCHES * D_MODEL), tok),
        pl.BlockSpec((1, TT, D_MODEL), tok),
        pl.BlockSpec((1, 1, 6 * D_MODEL), lambda b, i: (b, 0, 0)),
        _once((1, len(POOL_WINDOWS), LANES, LANES), lambda b, i: (l, 0, 0, 0)),
        _once((1, 1, B_WIDTH), lay3), _once((1, CONV_W, D_WIDTH), lay3),
        _once((1, A_WIDTH, D_MODEL), lay3), _once((1, B_WIDTH, D_MODEL), lay3),
        _once((1, C_WIDTH, D_MODEL), lay3), _once((1, D_WIDTH, D_MODEL), lay3),
        _once((1, D_MODEL, D_MODEL), lay3),
        _once((1, 1, D_MODEL), lay3), _once((1, 1, D_MODEL), lay3),
        _once((1, N_EXPERTS, D_MODEL), lay3), _once((1, N_EXPERTS, 1), lay3),
    ]
    out_shape = [jax.ShapeDtypeStruct((B, T, D_MODEL), F32),
                 jax.ShapeDtypeStruct((B, T, D_MODEL), BF16),
                 jax.ShapeDtypeStruct((cfg.NT, N_EXPERTS, TT), F32)]
    out_specs = [pl.BlockSpec((1, TT, D_MODEL), tok), pl.BlockSpec((1, TT, D_MODEL), tok),
                 pl.BlockSpec((1, N_EXPERTS, TT), lambda b, i: (b * nT + i, 0, 0))]
    return pl.pallas_call(
        functools.partial(_merge_body, TT=TT, T=T),
        grid=(B, nT), in_specs=in_specs, out_specs=out_specs, out_shape=out_shape,
        compiler_params=_cparams(("parallel", "arbitrary")),
        name="merge",
    )(ya, yc, zb, zb, zb, zp, zp, zp, bg, gates, x, mod, b_wg, b_scale, d_conv,
      p_a, p_b, p_c, p_d, w_o, ln1_g, ln1_b, r_wT, r_b)


def _selected(aff_tile, tau, tie_end, tile, TT):
    bits = pltpu.bitcast(aff_tile, I32)
    tok = tile * TT + lax.broadcasted_iota(I32, bits.shape, 1)
    return (bits > tau) | ((bits == tau) & (tok < tie_end))


def _count(mask):
    return jnp.sum(jnp.sum(jnp.where(mask, 1.0, 0.0), axis=0), axis=1, keepdims=True)


def _select_body(aff_ref, tauP_ref, meta_ref, *, NT, TT, NTP, cap):
    bits = pltpu.bitcast(aff_ref[...], I32)
    capf = float(cap)

    def tau_step(k, prefix):
        cand = prefix | jnp.left_shift(jnp.int32(1), 30 - k)
        return jnp.where(_count(bits >= cand[None]) >= capf, cand, prefix)
    tau = lax.fori_loop(0, 31, tau_step, jnp.zeros((N_EXPERTS, 1), I32))

    need = capf - _count(bits > tau[None])
    tie = bits == tau[None]
    tok = (lax.broadcasted_iota(I32, bits.shape, 0) * TT + lax.broadcasted_iota(I32, bits.shape, 2))
    n_bits = int(NT * TT).bit_length()

    def tie_step(k, end):
        cand = end | jnp.left_shift(jnp.int32(1), n_bits - 1 - k)
        return jnp.where(_count(tie & (tok < cand[None])) <= need, cand, end)
    tie_end = lax.fori_loop(0, n_bits, tie_step, jnp.zeros((N_EXPERTS, 1), I32))

    lane = lax.broadcasted_iota(I32, (N_EXPERTS, LANES), 1)
    tauP_ref[...] = jnp.where(lane == 0, tau, jnp.where(lane == 1, tie_end, 0))

    col = lax.broadcasted_iota(I32, (N_EXPERTS, NTP), 1)

    def cnt_step(i, acc):
        sel = _selected(aff_ref[i], tau, tie_end, i, TT)
        c = jnp.sum(jnp.where(sel, 1.0, 0.0), axis=1, keepdims=True)
        return jnp.where(col == i, c, acc)
    counts = lax.fori_loop(0, NT, cnt_step, jnp.zeros((N_EXPERTS, NTP), F32)).astype(I32)
    padded = ((counts + (BF16_ROWS - 1)) // BF16_ROWS) * BF16_ROWS
    before = (lax.broadcasted_iota(I32, (NTP, NTP), 0) < lax.broadcasted_iota(I32, (NTP, NTP), 1))
    offs = _dot(padded.astype(BF16), jnp.where(before, 1.0, 0.0).astype(BF16))
    meta_ref[0:N_EXPERTS, :] = offs.astype(I32)
    meta_ref[N_EXPERTS:2 * N_EXPERTS, :] = counts


def _select(cfg, aff3):
    NT, TT, NTP = cfg.NT, cfg.TT, cfg.NTP
    return pl.pallas_call(
        functools.partial(_select_body, NT=NT, TT=TT, NTP=NTP, cap=cfg.cap),
        grid=(1,),
        in_specs=[pl.BlockSpec((NT, N_EXPERTS, TT), lambda i: (0, 0, 0))],
        out_specs=[pl.BlockSpec((N_EXPERTS, LANES), lambda i: (0, 0)),
                   pl.BlockSpec((2 * N_EXPERTS, NTP), lambda i: (0, 0))],
        out_shape=[jax.ShapeDtypeStruct((N_EXPERTS, LANES), I32),
                   jax.ShapeDtypeStruct((2 * N_EXPERTS, NTP), I32)],
        compiler_params=_cparams(("arbitrary",)),
        name="select",
    )(aff3)


def _tile_slots(aff_ref, tauP_ref, tile, TT):
    tau = tauP_ref[:, 0:1]
    tie_end = tauP_ref[:, 1:2]
    sel = _selected(aff_ref[0], tau, tie_end, tile, TT)
    before = (lax.broadcasted_iota(I32, (TT, TT), 0) < lax.broadcasted_iota(I32, (TT, TT), 1))
    rank = _dot(jnp.where(sel, 1.0, 0.0).astype(BF16), jnp.where(before, 1.0, 0.0).astype(BF16))
    return jnp.where(sel, rank, -1.0)


def _onehots(slots, r, TT, W):
    want = (r * W + lax.broadcasted_iota(I32, (W, TT), 0)).astype(F32)
    return jnp.concatenate([jnp.where(slots[e:e + 1, :] == want, 1.0, 0.0) for e in range(N_EXPERTS)], axis=0)


def _rounds(meta_ref, tile, W):
    n_max = meta_ref[N_EXPERTS, tile]
    for e in range(1, N_EXPERTS):
        n_max = jnp.maximum(n_max, meta_ref[N_EXPERTS + e, tile])
    return (n_max + (W - 1)) // W


def _window_live(meta_ref, e, tile, r, W):
    n = meta_ref[N_EXPERTS + e, tile]
    padded = ((n + (BF16_ROWS - 1)) // BF16_ROWS) * BF16_ROWS
    return r * W < padded


def _window_start(meta_ref, e, tile, r, W, capP):
    return pl.multiple_of(e * capP + meta_ref[e, tile] + r * W, BF16_ROWS)


def _dispatch_body(meta_ref, aff_ref, tauP_ref, h_ref, xg_in_ref, xg_ref, stage, extra, sem, *, TT, W, capP):
    del xg_in_ref
    i = pl.program_id(0)
    slot = i % 2
    slots = _tile_slots(aff_ref, tauP_ref, i, TT)

    def gathered(r):
        return _dot(_onehots(slots, r, TT, W).astype(BF16), h_ref[...]).astype(BF16)

    def first_copies(tile, buf):
        return [pltpu.make_async_copy(stage.at[buf, pl.ds(e * W, W)],
                                      xg_ref.at[pl.ds(_window_start(meta_ref, e, tile, 0, W, capP), W)],
                                      sem.at[buf]) for e in range(N_EXPERTS)]

    stage[slot] = gathered(0)

    @pl.when(i > 0)
    def _():
        for cp in first_copies(i - 1, 1 - slot):
            cp.wait()
    for cp in first_copies(i, slot):
        cp.start()

    def extra_round(r, carry):
        extra[...] = gathered(r)
        cps = [(_window_live(meta_ref, e, i, r, W),
                pltpu.make_async_copy(extra.at[pl.ds(e * W, W)],
                                      xg_ref.at[pl.ds(_window_start(meta_ref, e, i, r, W, capP), W)], sem.at[2]))
               for e in range(N_EXPERTS)]
        for live, cp in cps:
            @pl.when(live)
            def _():
                cp.start()
        for live, cp in cps:
            @pl.when(live)
            def _():
                cp.wait()
        return carry

    lax.fori_loop(1, _rounds(meta_ref, i, W), extra_round, 0)

    @pl.when(i == pl.num_programs(0) - 1)
    def _():
        for cp in first_copies(i, slot):
            cp.wait()


def _dispatch(cfg, meta, aff3, tauP, h2):
    NT, TT, W, capP = cfg.NT, cfg.TT, cfg.W, cfg.capP
    xg0 = jnp.zeros((N_EXPERTS * capP, D_MODEL), BF16)
    grid_spec = pltpu.PrefetchScalarGridSpec(
        num_scalar_prefetch=1, grid=(NT,),
        in_specs=[pl.BlockSpec((1, N_EXPERTS, TT), lambda i, m: (i, 0, 0)),
                  pl.BlockSpec((N_EXPERTS, LANES), lambda i, m: (0, 0)),
                  pl.BlockSpec((TT, D_MODEL), lambda i, m: (i, 0)),
                  pl.BlockSpec(memory_space=pl.ANY)],
        out_specs=pl.BlockSpec(memory_space=pl.ANY),
        scratch_shapes=[pltpu.VMEM((2, N_EXPERTS * W, D_MODEL), BF16), pltpu.VMEM((N_EXPERTS * W, D_MODEL), BF16),
                        pltpu.SemaphoreType.DMA((3,))])
    return pl.pallas_call(
        functools.partial(_dispatch_body, TT=TT, W=W, capP=capP),
        grid_spec=grid_spec,
        out_shape=jax.ShapeDtypeStruct((N_EXPERTS * capP, D_MODEL), BF16),
        input_output_aliases={4: 0},
        compiler_params=_cparams(("arbitrary",)),
        name="dispatch",
    )(meta, aff3, tauP, h2, xg0)


def _experts_body(meta_ref, xg_ref, w1_ref, w3_ref, w2_ref, o_ref, *, NT, TM, W):
    e = pl.program_id(0)
    k = pl.program_id(1)
    used = meta_ref[e, NT] + W

    @pl.when(k * TM < used)
    def _():
        x = xg_ref[...]
        a = _dot(x, w1_ref[0, 0])
        b = _dot(x, w3_ref[0, 0])
        o_ref[...] = _dot((a * _sigmoid(a) * b).astype(BF16), w2_ref[0, 0])

    @pl.when(k * TM >= used)
    def _():
        o_ref[...] = jnp.zeros(o_ref.shape, F32)


def _experts(cfg, l, meta, xg, w1, w3, w2):
    TM, capP = cfg.TM, cfg.capP
    nk = capP // TM
    wspec = pl.BlockSpec((1, 1, D_MODEL, D_FF_EXPERT), lambda e, k, m: (l, e, 0, 0))
    grid_spec = pltpu.PrefetchScalarGridSpec(
        num_scalar_prefetch=1, grid=(N_EXPERTS, nk),
        in_specs=[pl.BlockSpec((TM, D_MODEL), lambda e, k, m: (e * nk + k, 0)), wspec, wspec,
                  pl.BlockSpec((1, 1, D_FF_EXPERT, D_MODEL), lambda e, k, m: (l, e, 0, 0))],
        out_specs=pl.BlockSpec((TM, D_MODEL), lambda e, k, m: (e * nk + k, 0)))
    return pl.pallas_call(
        functools.partial(_experts_body, NT=cfg.NT, TM=TM, W=cfg.W),
        grid_spec=grid_spec,
        out_shape=jax.ShapeDtypeStruct((N_EXPERTS * capP, D_MODEL), F32),
        compiler_params=_cparams(("arbitrary", "arbitrary")),
        name="experts",
    )(meta, xg, w1, w3, w2)


def _combine_body(meta_ref, aff_ref, tauP_ref, x1_ref, mod_ref, ln2g_ref, ln2b_ref, og_ref,
                  x2_ref, rows, acc, sem, *, TT, W, capP):
    i = pl.program_id(0)
    buf = i % 2

    def first_copies(tile, b):
        return [pltpu.make_async_copy(og_ref.at[pl.ds(_window_start(meta_ref, e, tile, 0, W, capP), W)],
                                      rows.at[b, pl.ds(e * W, W)], sem.at[b]) for e in range(N_EXPERTS)]

    @pl.when(i == 0)
    def _():
        for cp in first_copies(0, 0):
            cp.start()

    @pl.when(i + 1 < pl.num_programs(0))
    def _():
        for cp in first_copies(i + 1, 1 - buf):
            cp.start()

    aff = aff_ref[0]
    slots = _tile_slots(aff_ref, tauP_ref, i, TT)

    def hot_and_gate(r):
        hot = _onehots(slots, r, TT, W)
        gate = jnp.concatenate(
            [jnp.sum(hot[e * W:(e + 1) * W] * aff[e:e + 1, :], axis=1, keepdims=True) for e in range(N_EXPERTS)], axis=0)
        return hot.astype(BF16), gate

    def expand(hot16, gate):
        scaled = rows[buf] * gate
        hi = scaled.astype(BF16)
        lo = (scaled - hi.astype(F32)).astype(BF16)
        tn = (((0,), (0,)), ((), ()))
        return (lax.dot_general(hot16, hi, tn, preferred_element_type=F32)
                + lax.dot_general(hot16, lo, tn, preferred_element_type=F32))

    hot16, gate = hot_and_gate(0)
    for cp in first_copies(i, buf):
        cp.wait()
    acc[...] = expand(hot16, gate)

    def extra_round(r, carry):
        cps = [(_window_live(meta_ref, e, i, r, W),
                pltpu.make_async_copy(og_ref.at[pl.ds(_window_start(meta_ref, e, i, r, W, capP), W)],
                                      rows.at[buf, pl.ds(e * W, W)], sem.at[2])) for e in range(N_EXPERTS)]
        for e, (live, cp) in enumerate(cps):
            @pl.when(live)
            def _():
                cp.start()

            @pl.when(jnp.logical_not(live))
            def _():
                rows[buf, pl.ds(e * W, W), :] = jnp.zeros((W, D_MODEL), F32)
        hot16, gate = hot_and_gate(r)
        for live, cp in cps:
            @pl.when(live)
            def _():
                cp.wait()
        acc[...] += expand(hot16, gate)
        return carry

    lax.fori_loop(1, _rounds(meta_ref, i, W), extra_round, 0)
    g2 = mod_ref[0][:, 5 * D_MODEL:6 * D_MODEL]
    x2_ref[0] = _ln(ALPHA * x1_ref[0] + g2 * acc[...]) * ln2g_ref[0] + ln2b_ref[0]


def _combine(cfg, l, meta, aff3, tauP, x1, mod, ln2_g, ln2_b, og):
    B, T, TT, nT, NT, W, capP = cfg.B, cfg.T, cfg.TT, cfg.nT, cfg.NT, cfg.W, cfg.capP
    tok = lambda i, m: (i // nT, i % nT, 0)
    lay3 = lambda i, m: (l, 0, 0)
    grid_spec = pltpu.PrefetchScalarGridSpec(
        num_scalar_prefetch=1, grid=(NT,),
        in_specs=[pl.BlockSpec((1, N_EXPERTS, TT), lambda i, m: (i, 0, 0)),
                  pl.BlockSpec((N_EXPERTS, LANES), lambda i, m: (0, 0)),
                  pl.BlockSpec((1, TT, D_MODEL), tok),
                  pl.BlockSpec((1, 1, 6 * D_MODEL), lambda i, m: (i // nT, 0, 0)),
                  pl.BlockSpec((1, 1, D_MODEL), lay3), pl.BlockSpec((1, 1, D_MODEL), lay3),
                  pl.BlockSpec(memory_space=pl.ANY)],
        out_specs=pl.BlockSpec((1, TT, D_MODEL), tok),
        scratch_shapes=[pltpu.VMEM((2, N_EXPERTS * W, D_MODEL), F32), pltpu.VMEM((TT, D_MODEL), F32),
                        pltpu.SemaphoreType.DMA((3,))])
    return pl.pallas_call(
        functools.partial(_combine_body, TT=TT, W=W, capP=capP),
        grid_spec=grid_spec,
        out_shape=jax.ShapeDtypeStruct((B, T, D_MODEL), F32),
        compiler_params=_cparams(("arbitrary",)),
        name="combine",
    )(meta, aff3, tauP, x1, mod, ln2_g, ln2_b, og)


def _rope_tables(T):
    n_rows = T // GRID_W
    row = jnp.repeat(jnp.arange(n_rows, dtype=F32), GRID_W)
    col = jnp.tile(jnp.arange(GRID_W, dtype=F32), n_rows)
    inv = ROPE_THETA ** (-jnp.arange(ROPE_PAIRS_AXIS, dtype=F32) / ROPE_PAIRS_AXIS)
    ang = jnp.concatenate([row[:, None] * inv, col[:, None] * inv], axis=-1)
    cos = jnp.repeat(jnp.cos(ang), 2, axis=1)
    sin = jnp.repeat(jnp.sin(ang), 2, axis=1) * jnp.tile(jnp.array([-1.0, 1.0], F32), HEAD_DIM // 2)
    pad = ((0, 0), (0, HP - HEAD_DIM))
    return jnp.pad(cos, pad), jnp.pad(sin, pad)


def _pad_heads(w, n_heads):
    L = w.shape[0]
    w = w.reshape(L, D_MODEL, n_heads, HEAD_DIM)
    w = jnp.pad(w, ((0, 0), (0, 0), (0, 0), (0, HP - HEAD_DIM)))
    return w.reshape(L, D_MODEL, n_heads * HP)


def _prep_weights(w_in, a_ln_g, a_ln_b, a_ws, a_bs, b_wg, b_scale, c_q_g, c_k_g, d_conv,
                  p_a, p_b, p_c, p_d, w_o, ln1_g, ln1_b, r_w, r_b, e_w1, e_w3, e_w2, ln2_g, ln2_b):
    L = w_in.shape[0]
    sec_a = w_in[:, :, OFF_A:OFF_B]
    sec_b = w_in[:, :, OFF_B:OFF_C]
    sec_q = _pad_heads(w_in[:, :, OFF_C:OFF_C + C_WIDTH], N_HEADS)
    sec_k = _pad_heads(w_in[:, :, OFF_C + C_WIDTH:OFF_C + C_WIDTH + KV_WIDTH], N_KV_HEADS)
    sec_v = _pad_heads(w_in[:, :, OFF_C + C_WIDTH + KV_WIDTH:OFF_D], N_KV_HEADS)
    sec_d = w_in[:, :, OFF_D:OFF_G]
    sec_g = w_in[:, :, OFF_G:]
    w_in_p = jnp.concatenate([sec_g, sec_a, sec_q, sec_d, sec_b, sec_k, sec_v], axis=-1).astype(BF16)
    row = lambda a: a.reshape(L, 1, -1)
    head_gain = lambda g: jnp.pad(g, ((0, 0), (0, HP - HEAD_DIM))).reshape(L, 1, HP)
    s_bound = 1.01 * (HEAD_DIM ** 0.5) * jnp.max(jnp.abs(c_q_g), axis=1) * jnp.max(jnp.abs(c_k_g), axis=1)
    k_off = jnp.zeros((L, 1, HP), F32).at[:, 0, HEAD_DIM].set(-s_bound)
    return dict(
        s_bound=s_bound, k_off=k_off,
        w_in_p=w_in_p, a_ln_g=row(a_ln_g), a_ln_b=row(a_ln_b), a_ws=a_ws.astype(BF16),
        a_bsT=jnp.swapaxes(a_bs, 1, 2), b_wg=b_wg.astype(BF16), b_scale=row(b_scale),
        q_g=head_gain(c_q_g), k_g=head_gain(c_k_g), d_conv=d_conv,
        p_a=p_a.astype(BF16), p_b=p_b.astype(BF16), p_c=p_c.astype(BF16), p_d=p_d.astype(BF16),
        w_o=w_o.astype(BF16), ln1_g=row(ln1_g), ln1_b=row(ln1_b),
        r_wT=jnp.swapaxes(r_w, 1, 2).astype(BF16), r_b=r_b.reshape(L, N_EXPERTS, 1),
        e_w1=e_w1.astype(BF16), e_w3=e_w3.astype(BF16), e_w2=e_w2.astype(BF16),
        ln2_g=row(ln2_g), ln2_b=row(ln2_b))


def _trunk(cfg, x, mod_all, p):
    cos_t, sin_t = _rope_tables(cfg.T)
    for l in range(DEPTH):
        mod = mod_all[l]
        ya, zb, zp, bg, qp, kT, vp, gates = _in_proj(
            cfg, l, x, mod, p["w_in_p"], p["a_ln_g"], p["a_ln_b"], p["a_ws"], p["a_bsT"],
            p["q_g"], p["k_g"], p["k_off"], cos_t, sin_t)
        yc = lax.cond(p["s_bound"][l] <= MAX_S_BOUND,
                      functools.partial(_attention, cfg, bounded=True),
                      functools.partial(_attention, cfg, bounded=False), qp, kT, vp)
        x1, h2, aff3 = _merge(cfg, l, ya, yc, zb, zp, bg, gates, x, mod, p["b_wg"], p["b_scale"], p["d_conv"],
                              p["p_a"], p["p_b"], p["p_c"], p["p_d"], p["w_o"], p["ln1_g"], p["ln1_b"],
                              p["r_wT"], p["r_b"])
        tauP, meta = _select(cfg, aff3)
        xg = _dispatch(cfg, meta, aff3, tauP, h2.reshape(cfg.n_tok, D_MODEL))
        og = _experts(cfg, l, meta, xg, p["e_w1"], p["e_w3"], p["e_w2"])
        x = _combine(cfg, l, meta, aff3, tauP, x1, mod, p["ln2_g"], p["ln2_b"], og)
    return x


def kernel(x_prompt, x_sample, c_prompt, c_sample, w_ada, b_ada, w_in, a_ln_g, a_ln_b, a_ws, a_bs, b_wg, b_scale, c_q_g, c_k_g, d_conv, p_a, p_b, p_c, p_d, w_o, ln1_g, ln1_b, r_w, r_b, e_w1, e_w3, e_w2, ln2_g, ln2_b):
    p = _prep_weights(w_in, a_ln_g, a_ln_b, a_ws, a_bs, b_wg, b_scale, c_q_g, c_k_g, d_conv,
                      p_a, p_b, p_c, p_d, w_o, ln1_g, ln1_b, r_w, r_b, e_w1, e_w3, e_w2, ln2_g, ln2_b)
    Bp, Bs = x_prompt.shape[0], x_sample.shape[0]
    rows = -(-(Bp + Bs) // SUBLANES) * SUBLANES
    c_all = jnp.pad(jnp.concatenate([c_prompt, c_sample], axis=0), ((0, rows - Bp - Bs), (0, 0)))
    mod = _modulation(c_all, w_ada, b_ada)
    mod_p = mod[:, :Bp, None, :]
    mod_s = mod[:, Bp:Bp + Bs, None, :]
    y_prompt = _trunk(make_cfg(*x_prompt.shape[:2]), x_prompt, mod_p, p)
    y_sample = _trunk(make_cfg(*x_sample.shape[:2]), x_sample, mod_s, p)
    return (y_prompt, y_sample)
```

```python
import functools
from typing import NamedTuple

import numpy as np
import jax
import jax.numpy as jnp
from jax import lax
from jax.experimental import pallas as pl
from jax.experimental.pallas import tpu as pltpu

F32 = jnp.float32
BF16 = jnp.bfloat16
I32 = jnp.int32

D_MODEL = 1024
DEPTH = 4
GRID_W = 64
CHUNK = 128
A_GROUPS = 4
A_WIDTH = 512
POOL_WINDOWS = (2, 4, 8, 16)
B_WIDTH = 512
N_HEADS = 8
N_KV_HEADS = 2
HEAD_DIM = 64
C_WIDTH = N_HEADS * HEAD_DIM
KV_WIDTH = N_KV_HEADS * HEAD_DIM
ROPE_THETA = 10000.0
ROPE_PAIRS_AXIS = HEAD_DIM // 4
D_WIDTH = 512
CONV_W = 3
N_BRANCHES = 4
N_EXPERTS = 16
D_FF_EXPERT = 1024
EC_CAPACITY = 2
ALPHA = (2 * DEPTH) ** 0.25
LN_EPS = 1e-5
RMS_EPS = 1e-6

OFF_A = 0
OFF_B = OFF_A + 2 * A_WIDTH
OFF_C = OFF_B + B_WIDTH
OFF_D = OFF_C + C_WIDTH + 2 * KV_WIDTH
OFF_G = OFF_D + 3 * D_WIDTH
N_IN = OFF_G + N_BRANCHES * D_MODEL

LANES = 128
SUBLANES = 8
BF16_ROWS = 16
VMEM_LIMIT = 56 * 1024 * 1024

MAX_S_BOUND = 40.0
HP = LANES
HALO = SUBLANES


class Cfg(NamedTuple):
    B: int
    T: int
    TT: int
    TQ: int
    TKC: int
    W: int
    TM: int

    @property
    def n_tok(self): return self.B * self.T
    @property
    def nT(self): return self.T // self.TT
    @property
    def NT(self): return self.n_tok // self.TT
    @property
    def NTP(self): return -(-(self.NT + 1) // LANES) * LANES
    @property
    def cap(self): return EC_CAPACITY * self.n_tok // N_EXPERTS
    @property
    def capP(self):
        worst = self.cap + (BF16_ROWS - 1) * self.NT + self.W
        return -(-worst // self.TM) * self.TM


def make_cfg(B, T):
    return Cfg(B=B, T=T, TT=256, TQ=256, TKC=1024, W=64, TM=512)


def _cparams(sem):
    return pltpu.CompilerParams(dimension_semantics=sem, vmem_limit_bytes=VMEM_LIMIT)


def _ln(x):
    mu = jnp.mean(x, axis=-1, keepdims=True)
    xc = x - mu
    var = jnp.mean(xc * xc, axis=-1, keepdims=True)
    return xc * lax.rsqrt(var + LN_EPS)


def _gelu_tanh(x):
    return 0.5 * x * (1.0 + jnp.tanh(np.sqrt(2.0 / np.pi).astype(np.float32) * (x + 0.044715 * (x * x * x))))


def _sigmoid(x):
    return 1.0 / (1.0 + jnp.exp(-x))


def _dot(a, b):
    return jnp.dot(a, b, preferred_element_type=F32)


def _once(block, index_map):
    return pl.BlockSpec(block, index_map, pipeline_mode=pl.Buffered(1))


def _mod_body(c_ref, w_ref, b_ref, o_ref):
    c = c_ref[...]
    s = (c * _sigmoid(c)).astype(BF16)
    o_ref[0] = _dot(s, w_ref[0].astype(BF16)) + b_ref[0]


def _modulation(c_all, w_ada, b_ada):
    R = c_all.shape[0]
    L = w_ada.shape[0]
    nb = 1536
    return pl.pallas_call(
        _mod_body,
        grid=(L, 6 * D_MODEL // nb),
        in_specs=[pl.BlockSpec((R, D_MODEL), lambda l, j: (0, 0)),
                  pl.BlockSpec((1, D_MODEL, nb), lambda l, j: (l, 0, j)),
                  pl.BlockSpec((1, 1, nb), lambda l, j: (l, 0, j))],
        out_specs=pl.BlockSpec((1, R, nb), lambda l, j: (l, 0, j)),
        out_shape=jax.ShapeDtypeStruct((L, R, 6 * D_MODEL), F32),
        compiler_params=_cparams(("arbitrary", "arbitrary")),
        name="modulation",
    )(c_all, w_ada, b_ada.reshape(L, 1, 6 * D_MODEL))


def _rms_rope(z, g, cosb, sinb, even):
    ms = jnp.sum(z * z, axis=-1, keepdims=True) * (1.0 / HEAD_DIM)
    zn = z * lax.rsqrt(ms + RMS_EPS) * g
    partner = jnp.where(even, pltpu.roll(zn, HP - 1, 1), pltpu.roll(zn, 1, 1))
    return zn * cosb + partner * sinb


def _in_proj_body(x_ref, mod_ref, wa_ref, wq_ref, wg_ref, wd_ref, wb_ref, wk_ref, wv_ref,
                  alng_ref, alnb_ref, aws_ref, absT_ref, qg_ref, kg_ref, koff_ref, cos_ref, sin_ref,
                  ya_ref, zb_ref, zp_ref, bg_ref, qp_ref, kT_ref, vp_ref, gates_ref, *, TT):
    x = x_ref[0]
    mod = mod_ref[0]
    sh1 = mod[:, 0:D_MODEL]
    sc1 = mod[:, D_MODEL:2 * D_MODEL]
    h = (_ln(x) * (1.0 + sc1) + sh1).astype(BF16)

    def gate_cols(c):
        return _dot(h, wg_ref[0, :, c * D_MODEL:(c + 1) * D_MODEL])

    def put_gate(c, z):
        gates_ref[0, :, c * D_MODEL:(c + 1) * D_MODEL] = _sigmoid(z)

    za = _dot(h, wa_ref[0])
    zq = _dot(h, wq_ref[0])

    ga = _gelu_tanh(za)
    u = ga[:, :A_WIDTH]
    vn = (_ln(ga[:, A_WIDTH:]) * alng_ref[0] + alnb_ref[0]).astype(BF16)
    bsT = absT_ref[0]
    gc = A_WIDTH // A_GROUPS
    for n in range(TT // CHUNK):
        rows = slice(n * CHUNK, (n + 1) * CHUNK)
        for g in range(A_GROUPS):
            cols = slice(g * gc, (g + 1) * gc)
            sv = _dot(aws_ref[0, g], vn[rows, cols]) + bsT[:, g:g + 1]
            ya_ref[0, rows, cols] = (u[rows, cols] * sv).astype(BF16)

    cosb = cos_ref[...]
    sinb = sin_ref[...]
    lane = lax.broadcasted_iota(I32, (TT, HP), 1)
    even = (lane & 1) == 0
    zg0 = gate_cols(0)
    for hh in range(N_HEADS):
        q = _rms_rope(zq[:, hh * HP:(hh + 1) * HP], qg_ref[0], cosb, sinb, even)
        q = jnp.where(lane == HEAD_DIM, 1.0, q * (HEAD_DIM ** -0.5))
        qp_ref[0, :, hh * HP:(hh + 1) * HP] = q.astype(BF16)
    zk = _dot(h, wk_ref[0])
    zv = _dot(h, wv_ref[0])
    zg1 = gate_cols(1)
    put_gate(0, zg0)
    for kv in range(N_KV_HEADS):
        k = _rms_rope(zk[:, kv * HP:(kv + 1) * HP], kg_ref[0], cosb, sinb, even) + koff_ref[0]
        kT_ref[0, kv] = k.T.astype(BF16)
        v = jnp.where(lane == HEAD_DIM, 1.0, zv[:, kv * HP:(kv + 1) * HP])
        vp_ref[0, kv] = v.astype(BF16)

    zd = _dot(h, wd_ref[0])
    zg2 = gate_cols(2)
    put_gate(1, zg1)
    zp_ref[0] = zd[:, 2 * D_WIDTH:] * zd[:, :D_WIDTH]
    bg_ref[0] = zd[:, D_WIDTH:2 * D_WIDTH]
    zb = _dot(h, wb_ref[0])
    zg3 = gate_cols(3)
    put_gate(2, zg2)
    zb_ref[0] = zb
    put_gate(3, zg3)


def _in_proj(cfg, l, x, mod, w_in_p, a_ln_g, a_ln_b, a_ws, a_bsT, q_g, k_g, k_off, cos_t, sin_t):
    B, T, TT, nT = cfg.B, cfg.T, cfg.TT, cfg.nT
    tok = lambda b, i: (b, i, 0)
    lay3 = lambda b, i: (l, 0, 0)
    wspec = lambda width, start: _once((1, D_MODEL, width), lambda b, i: (l, 0, start // width))
    in_specs = [
        pl.BlockSpec((1, TT, D_MODEL), tok),
        pl.BlockSpec((1, 1, 6 * D_MODEL), lambda b, i: (b, 0, 0)),
        wspec(1024, 4096), wspec(1024, 5120), wspec(4096, 0), wspec(1536, 6144),
        wspec(512, 7680), wspec(256, 8192), wspec(256, 8448),
        _once((1, 1, A_WIDTH), lay3), _once((1, 1, A_WIDTH), lay3),
        _once((1, A_GROUPS, CHUNK, CHUNK), lambda b, i: (l, 0, 0, 0)),
        _once((1, CHUNK, A_GROUPS), lay3),
        _once((1, 1, HP), lay3), _once((1, 1, HP), lay3), _once((1, 1, HP), lay3),
        pl.BlockSpec((TT, HP), lambda b, i: (i, 0)), pl.BlockSpec((TT, HP), lambda b, i: (i, 0)),
    ]
    out_shape = [
        jax.ShapeDtypeStruct((B, T, A_WIDTH), BF16),
        jax.ShapeDtypeStruct((B, T, B_WIDTH), F32),
        jax.ShapeDtypeStruct((B, T, D_WIDTH), F32),
        jax.ShapeDtypeStruct((B, T, D_WIDTH), F32),
        jax.ShapeDtypeStruct((B, T, N_HEADS * HP), BF16),
        jax.ShapeDtypeStruct((B, N_KV_HEADS, HP, T), BF16),
        jax.ShapeDtypeStruct((B, N_KV_HEADS, T, HP), BF16),
        jax.ShapeDtypeStruct((B, T, N_BRANCHES * D_MODEL), F32),
    ]
    out_specs = [
        pl.BlockSpec((1, TT, A_WIDTH), tok), pl.BlockSpec((1, TT, B_WIDTH), tok),
        pl.BlockSpec((1, TT, D_WIDTH), tok), pl.BlockSpec((1, TT, D_WIDTH), tok),
        pl.BlockSpec((1, TT, N_HEADS * HP), tok),
        pl.BlockSpec((1, N_KV_HEADS, HP, TT), lambda b, i: (b, 0, 0, i)),
        pl.BlockSpec((1, N_KV_HEADS, TT, HP), lambda b, i: (b, 0, i, 0)),
        pl.BlockSpec((1, TT, N_BRANCHES * D_MODEL), tok),
    ]
    return pl.pallas_call(
        functools.partial(_in_proj_body, TT=TT),
        grid=(B, nT), in_specs=in_specs, out_specs=out_specs, out_shape=out_shape,
        compiler_params=_cparams(("parallel", "parallel")),
        name="in_proj",
    )(x, mod, w_in_p, w_in_p, w_in_p, w_in_p, w_in_p, w_in_p, w_in_p,
      a_ln_g, a_ln_b, a_ws, a_bsT, q_g, k_g, k_off, cos_t, sin_t)


def _attn_finish(acc_sc, o_ref, TQ):
    lane = lax.broadcasted_iota(I32, (TQ, HP), 1)
    for pp in range(N_HEADS // 2):
        acc_even = acc_sc[2 * pp]
        acc_odd = acc_sc[2 * pp + 1]
        o_even = acc_even / acc_even[:, HEAD_DIM:HEAD_DIM + 1]
        o_odd = acc_odd / acc_odd[:, HEAD_DIM:HEAD_DIM + 1]
        pair = jnp.where(lane < HEAD_DIM, o_even, pltpu.roll(o_odd, HEAD_DIM, 1))
        o_ref[0, :, pp * HP:(pp + 1) * HP] = pair.astype(BF16)


def _attn_bounded_body(q_ref, kT_ref, v_ref, o_ref, acc_sc, *, TQ, TKC, T):
    G = N_HEADS // N_KV_HEADS
    acc_sc[...] = jnp.zeros(acc_sc.shape, F32)

    def step(j, carry):
        off = pl.multiple_of(j * TKC, TKC)
        scores = lambda h: _dot(q_ref[0, :, h * HP:(h + 1) * HP], kT_ref[0, h // G, :, pl.ds(off, TKC)])
        s_next = scores(0)
        for h in range(N_HEADS):
            s = s_next
            if h + 1 < N_HEADS:
                s_next = scores(h + 1)
            acc_sc[h] += _dot(jnp.exp(s).astype(BF16), v_ref[0, h // G, pl.ds(off, TKC), :])
        return carry

    lax.fori_loop(0, T // TKC, step, 0)
    _attn_finish(acc_sc, o_ref, TQ)


def _attn_body(q_ref, kT_ref, v_ref, o_ref, m_sc, acc_sc, *, TQ, TKC, T):
    G = N_HEADS // N_KV_HEADS
    m_sc[...] = jnp.full(m_sc.shape, -jnp.inf, F32)
    acc_sc[...] = jnp.zeros(acc_sc.shape, F32)

    def step(j, carry):
        off = pl.multiple_of(j * TKC, TKC)
        for kv in range(N_KV_HEADS):
            kt = kT_ref[0, kv, :, pl.ds(off, TKC)]
            vv = v_ref[0, kv, pl.ds(off, TKC), :]
            for g in range(G):
                h = kv * G + g
                s = _dot(q_ref[0, :, h * HP:(h + 1) * HP], kt)
                m_prev = m_sc[h]
                m_new = jnp.maximum(m_prev, jnp.max(s, axis=1, keepdims=True))
                p = jnp.exp(s - m_new).astype(BF16)
                acc_sc[h] = jnp.exp(m_prev - m_new) * acc_sc[h] + _dot(p, vv)
                m_sc[h] = m_new
        return carry

    lax.fori_loop(0, T // TKC, step, 0)
    _attn_finish(acc_sc, o_ref, TQ)


def _attention(cfg, qp, kT, vp, bounded):
    B, T, TQ, TKC = cfg.B, cfg.T, cfg.TQ, cfg.TKC
    acc = pltpu.VMEM((N_HEADS, TQ, HP), F32)
    if bounded:
        body, scratch, name = _attn_bounded_body, [acc], "attention_bounded"
    else:
        body, scratch, name = _attn_body, [pltpu.VMEM((N_HEADS, TQ, 1), F32), acc], "attention"
    return pl.pallas_call(
        functools.partial(body, TQ=TQ, TKC=TKC, T=T),
        grid=(B, T // TQ),
        in_specs=[pl.BlockSpec((1, TQ, N_HEADS * HP), lambda b, i: (b, i, 0)),
                  pl.BlockSpec((1, N_KV_HEADS, HP, T), lambda b, i: (b, 0, 0, 0)),
                  pl.BlockSpec((1, N_KV_HEADS, T, HP), lambda b, i: (b, 0, 0, 0))],
        out_specs=pl.BlockSpec((1, TQ, C_WIDTH), lambda b, i: (b, i, 0)),
        out_shape=jax.ShapeDtypeStruct((B, T, C_WIDTH), BF16),
        scratch_shapes=scratch,
        compiler_params=_cparams(("parallel", "arbitrary")),
        name=name,
    )(qp, kT, vp)


def _merge_body(ya_ref, yc_ref, zb_ref, zb_prev_ref, zb_next_ref, zp_ref, zp_prev_ref, zp_next_ref,
                bg_ref, gates_ref, x_ref, mod_ref, bwg_ref, bscale_ref, dconv_ref,
                pa_ref, pb_ref, pc_ref, pd_ref, wo_ref, ln1g_ref, ln1b_ref, rwT_ref, rb_ref,
                x1_ref, h2_ref, aff_ref, *, TT, T):
    i = pl.program_id(1)
    first = i == 0
    last = i == pl.num_programs(1) - 1
    R = TT + 2 * HALO

    def with_halo(cur_ref, prev_ref, next_ref):
        prev = jnp.where(first, 0.0, prev_ref[0])
        nxt = jnp.where(last, 0.0, next_ref[0])
        return jnp.concatenate([prev, cur_ref[0], nxt], axis=0)

    up = lambda a, k: pltpu.roll(a, k, 0)
    down = lambda a, k: pltpu.roll(a, R - k, 0)

    zbe = with_halo(zb_ref, zb_prev_ref, zb_next_ref)
    t = i * TT + lax.broadcasted_iota(I32, (TT, 1), 0)
    gc = B_WIDTH // len(POOL_WINDOWS)
    mixed = []
    for g, w in enumerate(POOL_WINDOWS):
        e = zbe[:, g * gc:(g + 1) * gc]
        s = up(e, 1) + e
        half = 1
        while 2 * half < w:
            s = up(s, half) + down(s, half)
            half *= 2
        cnt = (jnp.minimum(t + w // 2, T) - jnp.maximum(t - w // 2, 0)).astype(F32)
        pooled = s[HALO:HALO + TT] / cnt - e[HALO:HALO + TT]
        mixed.append(_dot(pooled.astype(BF16), bwg_ref[0, g]))
    yb = jnp.concatenate(mixed, axis=1) * bscale_ref[0]

    zpe = with_halo(zp_ref, zp_prev_ref, zp_next_ref)
    cw = dconv_ref[0]
    conv = up(zpe, 1) * cw[0:1] + zpe * cw[1:2] + down(zpe, 1) * cw[2:3]
    yd = bg_ref[0] * conv[HALO:HALO + TT]

    mod = mod_ref[0]
    g1 = mod[:, 2 * D_MODEL:3 * D_MODEL]
    sh2 = mod[:, 3 * D_MODEL:4 * D_MODEL]
    sc2 = mod[:, 4 * D_MODEL:5 * D_MODEL]
    merged = gates_ref[0, :, 0:D_MODEL] * _dot(ya_ref[0], pa_ref[0])
    merged += gates_ref[0, :, D_MODEL:2 * D_MODEL] * _dot(yb.astype(BF16), pb_ref[0])
    merged += gates_ref[0, :, 2 * D_MODEL:3 * D_MODEL] * _dot(yc_ref[0], pc_ref[0])
    merged += gates_ref[0, :, 3 * D_MODEL:4 * D_MODEL] * _dot(yd.astype(BF16), pd_ref[0])
    y = _dot(merged.astype(BF16), wo_ref[0])
    x1 = _ln(ALPHA * x_ref[0] + g1 * y) * ln1g_ref[0] + ln1b_ref[0]
    x1_ref[0] = x1

    h2 = (_ln(x1) * (1.0 + sc2) + sh2).astype(BF16)
    h2_ref[0] = h2
    logits = lax.dot_general(rwT_ref[0], h2, (((1,), (1,)), ((), ())), preferred_element_type=F32) + rb_ref[0]
    ex = jnp.exp(logits - jnp.max(logits, axis=0, keepdims=True))
    aff_ref[0] = ex / jnp.sum(ex, axis=0, keepdims=True)


def _merge(cfg, l, ya, yc, zb, zp, bg, gates, x, mod, b_wg, b_scale, d_conv, p_a, p_b, p_c, p_d, w_o,
           ln1_g, ln1_b, r_wT, r_b):
    B, T, TT, nT = cfg.B, cfg.T, cfg.TT, cfg.nT
    hb = TT // HALO
    tok = lambda b, i: (b, i, 0)
    prev = lambda b, i: (b, jnp.maximum(i * hb - 1, 0), 0)
    nxt = lambda b, i: (b, jnp.minimum((i + 1) * hb, T // HALO - 1), 0)
    lay3 = lambda b, i: (l, 0, 0)
    in_specs = [
        pl.BlockSpec((1, TT, A_WIDTH), tok), pl.BlockSpec((1, TT, C_WIDTH), tok),
        pl.BlockSpec((1, TT, B_WIDTH), tok), pl.BlockSpec((1, HALO, B_WIDTH), prev), pl.BlockSpec((1, HALO, B_WIDTH), nxt),
        pl.BlockSpec((1, TT, D_WIDTH), tok), pl.BlockSpec((1, HALO, D_WIDTH), prev), pl.BlockSpec((1, HALO, D_WIDTH), nxt),
        pl.BlockSpec((1, TT, D_WIDTH), tok),
        pl.BlockSpec((1, TT, N_BRANCHES * D_MODEL), tok),
        pl.BlockSpec((1, TT, D_MODEL), tok),
        pl.BlockSpec((1, 1, 6 * D_MODEL), lambda b, i: (b, 0, 0)),
        _once((1, len(POOL_WINDOWS), LANES, LANES), lambda b, i: (l, 0, 0, 0)),
        _once((1, 1, B_WIDTH), lay3), _once((1, CONV_W, D_WIDTH), lay3),
        _once((1, A_WIDTH, D_MODEL), lay3), _once((1, B_WIDTH, D_MODEL), lay3),
        _once((1, C_WIDTH, D_MODEL), lay3), _once((1, D_WIDTH, D_MODEL), lay3),
        _once((1, D_MODEL, D_MODEL), lay3),
        _once((1, 1, D_MODEL), lay3), _once((1, 1, D_MODEL), lay3),
        _once((1, N_EXPERTS, D_MODEL), lay3), _once((1, N_EXPERTS, 1), lay3),
    ]
    out_shape = [jax.ShapeDtypeStruct((B, T, D_MODEL), F32),
                 jax.ShapeDtypeStruct((B, T, D_MODEL), BF16),
                 jax.ShapeDtypeStruct((cfg.NT, N_EXPERTS, TT), F32)]
    out_specs = [pl.BlockSpec((1, TT, D_MODEL), tok), pl.BlockSpec((1, TT, D_MODEL), tok),
                 pl.BlockSpec((1, N_EXPERTS, TT), lambda b, i: (b * nT + i, 0, 0))]
    return pl.pallas_call(
        functools.partial(_merge_body, TT=TT, T=T),
        grid=(B, nT), in_specs=in_specs, out_specs=out_specs, out_shape=out_shape,
        compiler_params=_cparams(("parallel", "arbitrary")),
        name="merge",
    )(ya, yc, zb, zb, zb, zp, zp, zp, bg, gates, x, mod, b_wg, b_scale, d_conv,
      p_a, p_b, p_c, p_d, w_o, ln1_g, ln1_b, r_wT, r_b)


def _selected(aff_tile, tau, tie_end, tile, TT):
    bits = pltpu.bitcast(aff_tile, I32)
    tok = tile * TT + lax.broadcasted_iota(I32, bits.shape, 1)
    return (bits > tau) | ((bits == tau) & (tok < tie_end))


def _count(mask):
    return jnp.sum(jnp.sum(jnp.where(mask, 1.0, 0.0), axis=0), axis=1, keepdims=True)


def _select_body(aff_ref, tauP_ref, meta_ref, *, NT, TT, NTP, cap):
    bits = pltpu.bitcast(aff_ref[...], I32)
    capf = float(cap)

    def tau_step(k, prefix):
        cand = prefix | jnp.left_shift(jnp.int32(1), 30 - k)
        return jnp.where(_count(bits >= cand[None]) >= capf, cand, prefix)
    tau = lax.fori_loop(0, 31, tau_step, jnp.zeros((N_EXPERTS, 1), I32))

    need = capf - _count(bits > tau[None])
    tie = bits == tau[None]
    tok = (lax.broadcasted_iota(I32, bits.shape, 0) * TT + lax.broadcasted_iota(I32, bits.shape, 2))
    n_bits = int(NT * TT).bit_length()

    def tie_step(k, end):
        cand = end | jnp.left_shift(jnp.int32(1), n_bits - 1 - k)
        return jnp.where(_count(tie & (tok < cand[None])) <= need, cand, end)
    tie_end = lax.fori_loop(0, n_bits, tie_step, jnp.zeros((N_EXPERTS, 1), I32))

    lane = lax.broadcasted_iota(I32, (N_EXPERTS, LANES), 1)
    tauP_ref[...] = jnp.where(lane == 0, tau, jnp.where(lane == 1, tie_end, 0))

    col = lax.broadcasted_iota(I32, (N_EXPERTS, NTP), 1)

    def cnt_step(i, acc):
        sel = _selected(aff_ref[i], tau, tie_end, i, TT)
        c = jnp.sum(jnp.where(sel, 1.0, 0.0), axis=1, keepdims=True)
        return jnp.where(col == i, c, acc)
    counts = lax.fori_loop(0, NT, cnt_step, jnp.zeros((N_EXPERTS, NTP), F32)).astype(I32)
    padded = ((counts + (BF16_ROWS - 1)) // BF16_ROWS) * BF16_ROWS
    before = (lax.broadcasted_iota(I32, (NTP, NTP), 0) < lax.broadcasted_iota(I32, (NTP, NTP), 1))
    offs = _dot(padded.astype(BF16), jnp.where(before, 1.0, 0.0).astype(BF16))
    meta_ref[0:N_EXPERTS, :] = offs.astype(I32)
    meta_ref[N_EXPERTS:2 * N_EXPERTS, :] = counts


def _select(cfg, aff3):
    NT, TT, NTP = cfg.NT, cfg.TT, cfg.NTP
    return pl.pallas_call(
        functools.partial(_select_body, NT=NT, TT=TT, NTP=NTP, cap=cfg.cap),
        grid=(1,),
        in_specs=[pl.BlockSpec((NT, N_EXPERTS, TT), lambda i: (0, 0, 0))],
        out_specs=[pl.BlockSpec((N_EXPERTS, LANES), lambda i: (0, 0)),
                   pl.BlockSpec((2 * N_EXPERTS, NTP), lambda i: (0, 0))],
        out_shape=[jax.ShapeDtypeStruct((N_EXPERTS, LANES), I32),
                   jax.ShapeDtypeStruct((2 * N_EXPERTS, NTP), I32)],
        compiler_params=_cparams(("arbitrary",)),
        name="select",
    )(aff3)


def _tile_slots(aff_ref, tauP_ref, tile, TT):
    tau = tauP_ref[:, 0:1]
    tie_end = tauP_ref[:, 1:2]
    sel = _selected(aff_ref[0], tau, tie_end, tile, TT)
    before = (lax.broadcasted_iota(I32, (TT, TT), 0) < lax.broadcasted_iota(I32, (TT, TT), 1))
    rank = _dot(jnp.where(sel, 1.0, 0.0).astype(BF16), jnp.where(before, 1.0, 0.0).astype(BF16))
    return jnp.where(sel, rank, -1.0)


def _onehots(slots, r, TT, W):
    want = (r * W + lax.broadcasted_iota(I32, (W, TT), 0)).astype(F32)
    return jnp.concatenate([jnp.where(slots[e:e + 1, :] == want, 1.0, 0.0) for e in range(N_EXPERTS)], axis=0)


def _rounds(meta_ref, tile, W):
    n_max = meta_ref[N_EXPERTS, tile]
    for e in range(1, N_EXPERTS):
        n_max = jnp.maximum(n_max, meta_ref[N_EXPERTS + e, tile])
    return (n_max + (W - 1)) // W


def _window_live(meta_ref, e, tile, r, W):
    n = meta_ref[N_EXPERTS + e, tile]
    padded = ((n + (BF16_ROWS - 1)) // BF16_ROWS) * BF16_ROWS
    return r * W < padded


def _window_start(meta_ref, e, tile, r, W, capP):
    return pl.multiple_of(e * capP + meta_ref[e, tile] + r * W, BF16_ROWS)


def _dispatch_body(meta_ref, aff_ref, tauP_ref, h_ref, xg_in_ref, xg_ref, stage, extra, sem, *, TT, W, capP):
    del xg_in_ref
    i = pl.program_id(0)
    slot = i % 2
    slots = _tile_slots(aff_ref, tauP_ref, i, TT)

    def gathered(r):
        return _dot(_onehots(slots, r, TT, W).astype(BF16), h_ref[...]).astype(BF16)

    def first_copies(tile, buf):
        return [pltpu.make_async_copy(stage.at[buf, pl.ds(e * W, W)],
                                      xg_ref.at[pl.ds(_window_start(meta_ref, e, tile, 0, W, capP), W)],
                                      sem.at[buf]) for e in range(N_EXPERTS)]

    stage[slot] = gathered(0)

    @pl.when(i > 0)
    def _():
        for cp in first_copies(i - 1, 1 - slot):
            cp.wait()
    for cp in first_copies(i, slot):
        cp.start()

    def extra_round(r, carry):
        extra[...] = gathered(r)
        cps = [(_window_live(meta_ref, e, i, r, W),
                pltpu.make_async_copy(extra.at[pl.ds(e * W, W)],
                                      xg_ref.at[pl.ds(_window_start(meta_ref, e, i, r, W, capP), W)], sem.at[2]))
               for e in range(N_EXPERTS)]
        for live, cp in cps:
            @pl.when(live)
            def _():
                cp.start()
        for live, cp in cps:
            @pl.when(live)
            def _():
                cp.wait()
        return carry

    lax.fori_loop(1, _rounds(meta_ref, i, W), extra_round, 0)

    @pl.when(i == pl.num_programs(0) - 1)
    def _():
        for cp in first_copies(i, slot):
            cp.wait()


def _dispatch(cfg, meta, aff3, tauP, h2):
    NT, TT, W, capP = cfg.NT, cfg.TT, cfg.W, cfg.capP
    xg0 = jnp.zeros((N_EXPERTS * capP, D_MODEL), BF16)
    grid_spec = pltpu.PrefetchScalarGridSpec(
        num_scalar_prefetch=1, grid=(NT,),
        in_specs=[pl.BlockSpec((1, N_EXPERTS, TT), lambda i, m: (i, 0, 0)),
                  pl.BlockSpec((N_EXPERTS, LANES), lambda i, m: (0, 0)),
                  pl.BlockSpec((TT, D_MODEL), lambda i, m: (i, 0)),
                  pl.BlockSpec(memory_space=pl.ANY)],
        out_specs=pl.BlockSpec(memory_space=pl.ANY),
        scratch_shapes=[pltpu.VMEM((2, N_EXPERTS * W, D_MODEL), BF16), pltpu.VMEM((N_EXPERTS * W, D_MODEL), BF16),
                        pltpu.SemaphoreType.DMA((3,))])
    return pl.pallas_call(
        functools.partial(_dispatch_body, TT=TT, W=W, capP=capP),
        grid_spec=grid_spec,
        out_shape=jax.ShapeDtypeStruct((N_EXPERTS * capP, D_MODEL), BF16),
        input_output_aliases={4: 0},
        compiler_params=_cparams(("arbitrary",)),
        name="dispatch",
    )(meta, aff3, tauP, h2, xg0)


def _experts_body(meta_ref, xg_ref, w1_ref, w3_ref, w2_ref, o_ref, *, NT, TM, W):
    e = pl.program_id(0)
    k = pl.program_id(1)
    used = meta_ref[e, NT] + W

    @pl.when(k * TM < used)
    def _():
        x = xg_ref[...]
        a = _dot(x, w1_ref[0, 0])
        b = _dot(x, w3_ref[0, 0])
        o_ref[...] = _dot((a * _sigmoid(a) * b).astype(BF16), w2_ref[0, 0])

    @pl.when(k * TM >= used)
    def _():
        o_ref[...] = jnp.zeros(o_ref.shape, F32)


def _experts(cfg, l, meta, xg, w1, w3, w2):
    TM, capP = cfg.TM, cfg.capP
    nk = capP // TM
    wspec = pl.BlockSpec((1, 1, D_MODEL, D_FF_EXPERT), lambda e, k, m: (l, e, 0, 0))
    grid_spec = pltpu.PrefetchScalarGridSpec(
        num_scalar_prefetch=1, grid=(N_EXPERTS, nk),
        in_specs=[pl.BlockSpec((TM, D_MODEL), lambda e, k, m: (e * nk + k, 0)), wspec, wspec,
                  pl.BlockSpec((1, 1, D_FF_EXPERT, D_MODEL), lambda e, k, m: (l, e, 0, 0))],
        out_specs=pl.BlockSpec((TM, D_MODEL), lambda e, k, m: (e * nk + k, 0)))
    return pl.pallas_call(
        functools.partial(_experts_body, NT=cfg.NT, TM=TM, W=cfg.W),
        grid_spec=grid_spec,
        out_shape=jax.ShapeDtypeStruct((N_EXPERTS * capP, D_MODEL), F32),
        compiler_params=_cparams(("arbitrary", "arbitrary")),
        name="experts",
    )(meta, xg, w1, w3, w2)


def _combine_body(meta_ref, aff_ref, tauP_ref, x1_ref, mod_ref, ln2g_ref, ln2b_ref, og_ref,
                  x2_ref, rows, acc, sem, *, TT, W, capP):
    i = pl.program_id(0)
    buf = i % 2

    def first_copies(tile, b):
        return [pltpu.make_async_copy(og_ref.at[pl.ds(_window_start(meta_ref, e, tile, 0, W, capP), W)],
                                      rows.at[b, pl.ds(e * W, W)], sem.at[b]) for e in range(N_EXPERTS)]

    @pl.when(i == 0)
    def _():
        for cp in first_copies(0, 0):
            cp.start()

    @pl.when(i + 1 < pl.num_programs(0))
    def _():
        for cp in first_copies(i + 1, 1 - buf):
            cp.start()

    aff = aff_ref[0]
    slots = _tile_slots(aff_ref, tauP_ref, i, TT)

    def hot_and_gate(r):
        hot = _onehots(slots, r, TT, W)
        gate = jnp.concatenate(
            [jnp.sum(hot[e * W:(e + 1) * W] * aff[e:e + 1, :], axis=1, keepdims=True) for e in range(N_EXPERTS)], axis=0)
        return hot.astype(BF16), gate

    def expand(hot16, gate):
        scaled = rows[buf] * gate
        hi = scaled.astype(BF16)
        lo = (scaled - hi.astype(F32)).astype(BF16)
        tn = (((0,), (0,)), ((), ()))
        return (lax.dot_general(hot16, hi, tn, preferred_element_type=F32)
                + lax.dot_general(hot16, lo, tn, preferred_element_type=F32))

    hot16, gate = hot_and_gate(0)
    for cp in first_copies(i, buf):
        cp.wait()
    acc[...] = expand(hot16, gate)

    def extra_round(r, carry):
        cps = [(_window_live(meta_ref, e, i, r, W),
                pltpu.make_async_copy(og_ref.at[pl.ds(_window_start(meta_ref, e, i, r, W, capP), W)],
                                      rows.at[buf, pl.ds(e * W, W)], sem.at[2])) for e in range(N_EXPERTS)]
        for e, (live, cp) in enumerate(cps):
            @pl.when(live)
            def _():
                cp.start()

            @pl.when(jnp.logical_not(live))
            def _():
                rows[buf, pl.ds(e * W, W), :] = jnp.zeros((W, D_MODEL), F32)
        hot16, gate = hot_and_gate(r)
        for live, cp in cps:
            @pl.when(live)
            def _():
                cp.wait()
        acc[...] += expand(hot16, gate)
        return carry

    lax.fori_loop(1, _rounds(meta_ref, i, W), extra_round, 0)
    g2 = mod_ref[0][:, 5 * D_MODEL:6 * D_MODEL]
    x2_ref[0] = _ln(ALPHA * x1_ref[0] + g2 * acc[...]) * ln2g_ref[0] + ln2b_ref[0]


def _combine(cfg, l, meta, aff3, tauP, x1, mod, ln2_g, ln2_b, og):
    B, T, TT, nT, NT, W, capP = cfg.B, cfg.T, cfg.TT, cfg.nT, cfg.NT, cfg.W, cfg.capP
    tok = lambda i, m: (i // nT, i % nT, 0)
    lay3 = lambda i, m: (l, 0, 0)
    grid_spec = pltpu.PrefetchScalarGridSpec(
        num_scalar_prefetch=1, grid=(NT,),
        in_specs=[pl.BlockSpec((1, N_EXPERTS, TT), lambda i, m: (i, 0, 0)),
                  pl.BlockSpec((N_EXPERTS, LANES), lambda i, m: (0, 0)),
                  pl.BlockSpec((1, TT, D_MODEL), tok),
                  pl.BlockSpec((1, 1, 6 * D_MODEL), lambda i, m: (i // nT, 0, 0)),
                  pl.BlockSpec((1, 1, D_MODEL), lay3), pl.BlockSpec((1, 1, D_MODEL), lay3),
                  pl.BlockSpec(memory_space=pl.ANY)],
        out_specs=pl.BlockSpec((1, TT, D_MODEL), tok),
        scratch_shapes=[pltpu.VMEM((2, N_EXPERTS * W, D_MODEL), F32), pltpu.VMEM((TT, D_MODEL), F32),
                        pltpu.SemaphoreType.DMA((3,))])
    return pl.pallas_call(
        functools.partial(_combine_body, TT=TT, W=W, capP=capP),
        grid_spec=grid_spec,
        out_shape=jax.ShapeDtypeStruct((B, T, D_MODEL), F32),
        compiler_params=_cparams(("arbitrary",)),
        name="combine",
    )(meta, aff3, tauP, x1, mod, ln2_g, ln2_b, og)


def _rope_tables(T):
    n_rows = T // GRID_W
    row = jnp.repeat(jnp.arange(n_rows, dtype=F32), GRID_W)
    col = jnp.tile(jnp.arange(GRID_W, dtype=F32), n_rows)
    inv = ROPE_THETA ** (-jnp.arange(ROPE_PAIRS_AXIS, dtype=F32) / ROPE_PAIRS_AXIS)
    ang = jnp.concatenate([row[:, None] * inv, col[:, None] * inv], axis=-1)
    cos = jnp.repeat(jnp.cos(ang), 2, axis=1)
    sin = jnp.repeat(jnp.sin(ang), 2, axis=1) * jnp.tile(jnp.array([-1.0, 1.0], F32), HEAD_DIM // 2)
    pad = ((0, 0), (0, HP - HEAD_DIM))
    return jnp.pad(cos, pad), jnp.pad(sin, pad)


def _pad_heads(w, n_heads):
    L = w.shape[0]
    w = w.reshape(L, D_MODEL, n_heads, HEAD_DIM)
    w = jnp.pad(w, ((0, 0), (0, 0), (0, 0), (0, HP - HEAD_DIM)))
    return w.reshape(L, D_MODEL, n_heads * HP)


def _prep_weights(w_in, a_ln_g, a_ln_b, a_ws, a_bs, b_wg, b_scale, c_q_g, c_k_g, d_conv,
                  p_a, p_b, p_c, p_d, w_o, ln1_g, ln1_b, r_w, r_b, e_w1, e_w3, e_w2, ln2_g, ln2_b):
    L = w_in.shape[0]
    sec_a = w_in[:, :, OFF_A:OFF_B]
    sec_b = w_in[:, :, OFF_B:OFF_C]
    sec_q = _pad_heads(w_in[:, :, OFF_C:OFF_C + C_WIDTH], N_HEADS)
    sec_k = _pad_heads(w_in[:, :, OFF_C + C_WIDTH:OFF_C + C_WIDTH + KV_WIDTH], N_KV_HEADS)
    sec_v = _pad_heads(w_in[:, :, OFF_C + C_WIDTH + KV_WIDTH:OFF_D], N_KV_HEADS)
    sec_d = w_in[:, :, OFF_D:OFF_G]
    sec_g = w_in[:, :, OFF_G:]
    w_in_p = jnp.concatenate([sec_g, sec_a, sec_q, sec_d, sec_b, sec_k, sec_v], axis=-1).astype(BF16)
    row = lambda a: a.reshape(L, 1, -1)
    head_gain = lambda g: jnp.pad(g, ((0, 0), (0, HP - HEAD_DIM))).reshape(L, 1, HP)
    s_bound = 1.01 * (HEAD_DIM ** 0.5) * jnp.max(jnp.abs(c_q_g), axis=1) * jnp.max(jnp.abs(c_k_g), axis=1)
    k_off = jnp.zeros((L, 1, HP), F32).at[:, 0, HEAD_DIM].set(-s_bound)
    return dict(
        s_bound=s_bound, k_off=k_off,
        w_in_p=w_in_p, a_ln_g=row(a_ln_g), a_ln_b=row(a_ln_b), a_ws=a_ws.astype(BF16),
        a_bsT=jnp.swapaxes(a_bs, 1, 2), b_wg=b_wg.astype(BF16), b_scale=row(b_scale),
        q_g=head_gain(c_q_g), k_g=head_gain(c_k_g), d_conv=d_conv,
        p_a=p_a.astype(BF16), p_b=p_b.astype(BF16), p_c=p_c.astype(BF16), p_d=p_d.astype(BF16),
        w_o=w_o.astype(BF16), ln1_g=row(ln1_g), ln1_b=row(ln1_b),
        r_wT=jnp.swapaxes(r_w, 1, 2).astype(BF16), r_b=r_b.reshape(L, N_EXPERTS, 1),
        e_w1=e_w1.astype(BF16), e_w3=e_w3.astype(BF16), e_w2=e_w2.astype(BF16),
        ln2_g=row(ln2_g), ln2_b=row(ln2_b))


def _trunk(cfg, x, mod_all, p):
    cos_t, sin_t = _rope_tables(cfg.T)
    for l in range(DEPTH):
        mod = mod_all[l]
        ya, zb, zp, bg, qp, kT, vp, gates = _in_proj(
            cfg, l, x, mod, p["w_in_p"], p["a_ln_g"], p["a_ln_b"], p["a_ws"], p["a_bsT"],
            p["q_g"], p["k_g"], p["k_off"], cos_t, sin_t)
        yc = lax.cond(p["s_bound"][l] <= MAX_S_BOUND,
                      functools.partial(_attention, cfg, bounded=True),
                      functools.partial(_attention, cfg, bounded=False), qp, kT, vp)
        x1, h2, aff3 = _merge(cfg, l, ya, yc, zb, zp, bg, gates, x, mod, p["b_wg"], p["b_scale"], p["d_conv"],
                              p["p_a"], p["p_b"], p["p_c"], p["p_d"], p["w_o"], p["ln1_g"], p["ln1_b"],
                              p["r_wT"], p["r_b"])
        tauP, meta = _select(cfg, aff3)
        xg = _dispatch(cfg, meta, aff3, tauP, h2.reshape(cfg.n_tok, D_MODEL))
        og = _experts(cfg, l, meta, xg, p["e_w1"], p["e_w3"], p["e_w2"])
        x = _combine(cfg, l, meta, aff3, tauP, x1, mod, p["ln2_g"], p["ln2_b"], og)
    return x


def kernel(x_prompt, x_sample, c_prompt, c_sample, w_ada, b_ada, w_in, a_ln_g, a_ln_b, a_ws, a_bs, b_wg, b_scale, c_q_g, c_k_g, d_conv, p_a, p_b, p_c, p_d, w_o, ln1_g, ln1_b, r_w, r_b, e_w1, e_w3, e_w2, ln2_g, ln2_b):
    p = _prep_weights(w_in, a_ln_g, a_ln_b, a_ws, a_bs, b_wg, b_scale, c_q_g, c_k_g, d_conv,
                      p_a, p_b, p_c, p_d, w_o, ln1_g, ln1_b, r_w, r_b, e_w1, e_w3, e_w2, ln2_g, ln2_b)
    Bp, Bs = x_prompt.shape[0], x_sample.shape[0]
    rows = -(-(Bp + Bs) // SUBLANES) * SUBLANES
    c_all = jnp.pad(jnp.concatenate([c_prompt, c_sample], axis=0), ((0, rows - Bp - Bs), (0, 0)))
    mod = _modulation(c_all, w_ada, b_ada)
    mod_p = mod[:, :Bp, None, :]
    mod_s = mod[:, Bp:Bp + Bs, None, :]
    y_prompt = _trunk(make_cfg(*x_prompt.shape[:2]), x_prompt, mod_p, p)
    y_sample = _trunk(make_cfg(*x_sample.shape[:2]), x_sample, mod_s, p)
    return (y_prompt, y_sample)
```
